```python
import math
import jax, jax.numpy as jnp
from jax import lax
import numpy as np

D_MODEL = 1024
BATCH = 4
SEQ = 8192
DEPTH = 2

HEAD_DIM = 64
BRANCH_W = D_MODEL // 4
N_HEADS_BR = BRANCH_W // HEAD_DIM
N_ATTN_HEADS = 2 * N_HEADS_BR
MOBA_BLOCK = 256
MOBA_TOPK = 3
MOBA_QCHUNK = 128
S5_GROUP = 16
S5_GROUPS = BRANCH_W // S5_GROUP
S5_STATE = 64
HGRN_CHUNK = 64
DIL_PAIRS = ((128, 1), (512, 4), (2048, 16))
DIL_BLOCK = 128
REL_BUCKETS = 32
REL_MAX_DIST = 2048
ALPHA = (2 * DEPTH) ** 0.25
BETA = (8 * DEPTH) ** -0.25
N_IN_SLOTS = 10
IN_COLS = N_IN_SLOTS * BRANCH_W + D_MODEL
LN_EPS = 1e-5
RMS_EPS = 1e-6
NEG = -1e30

kernel_name = 'hymba_moba_s5_hgrn2_dilated_deepnorm'


def rel_bucket(dist):
    max_exact = REL_BUCKETS // 2
    d = jnp.maximum(dist, 1).astype(jnp.float32)
    large = max_exact + (jnp.log(d / max_exact) / math.log(REL_MAX_DIST / max_exact)
                         * (REL_BUCKETS - max_exact)).astype(jnp.int32)
    return jnp.where(dist < max_exact, dist, jnp.minimum(large, REL_BUCKETS - 1))


def layer_norm(h, g, b):
    h = h.astype(jnp.float32)
    mu = h.mean(-1, keepdims=True)
    var = jnp.square(h - mu).mean(-1, keepdims=True)
    return (h - mu) * lax.rsqrt(var + LN_EPS) * g.astype(jnp.float32) + b.astype(jnp.float32)


def moba_attention(q, k, v, bias_table):
    Bt, H, S, Dh = q.shape
    scale = Dh ** -0.5
    Sp = -(-S // MOBA_BLOCK) * MOBA_BLOCK
    nkb = Sp // MOBA_BLOCK
    pad = ((0, 0), (0, 0), (0, Sp - S), (0, 0))
    kblk = jnp.pad(k, pad).reshape(Bt, H, nkb, MOBA_BLOCK, Dh)
    vblk = jnp.pad(v, pad).reshape(Bt, H, nkb, MOBA_BLOCK, Dh)
    kmean = kblk.astype(jnp.float32).mean(axis=3)
    topk = max(1, min(MOBA_TOPK, nkb - 1))
    nqc = S // MOBA_QCHUNK
    q_chunks = q.reshape(Bt, H, nqc, MOBA_QCHUNK, Dh).transpose(2, 0, 1, 3, 4)
    b_idx = jnp.arange(Bt)[:, None, None, None]
    h_idx = jnp.arange(H)[None, :, None, None]
    in_blk = jnp.arange(MOBA_BLOCK)
    blk_ids = jnp.arange(nkb)

    def one_chunk(args):
        qi, c = args
        qf = qi.astype(jnp.float32)
        pos = c * MOBA_QCHUNK + jnp.arange(MOBA_QCHUNK)
        own = (c * MOBA_QCHUNK) // MOBA_BLOCK
        gate = jnp.einsum('bhqd,bhnd->bhqn', qf, kmean)
        gate = jnp.where(blk_ids < own, gate, -jnp.inf)
        _, idx = lax.top_k(gate, topk)
        valid = idx < own
        ksel = kblk[b_idx, h_idx, idx].astype(jnp.float32)
        vsel = vblk[b_idx, h_idx, idx].astype(jnp.float32)
        l_sel = jnp.einsum('bhqd,bhqtkd->bhqtk', qf, ksel) * scale
        sel_pos = idx[..., None] * MOBA_BLOCK + in_blk
        sel_dist = jnp.maximum(pos[:, None, None] - sel_pos, 0)
        l_sel = l_sel + bias_table[rel_bucket(sel_dist), h_idx[..., None]].astype(jnp.float32)
        l_sel = jnp.where(valid[..., None], l_sel, NEG)
        kown = lax.dynamic_index_in_dim(kblk, own, axis=2, keepdims=False).astype(jnp.float32)
        vown = lax.dynamic_index_in_dim(vblk, own, axis=2, keepdims=False).astype(jnp.float32)
        own_dist = pos[:, None] - (own * MOBA_BLOCK + in_blk)[None, :]
        own_bias = bias_table[rel_bucket(jnp.maximum(own_dist, 0))].astype(jnp.float32).transpose(2, 0, 1)
        l_own = jnp.einsum('bhqd,bhkd->bhqk', qf, kown) * scale + own_bias
        l_own = jnp.where(own_dist >= 0, l_own, NEG)
        n_sel = topk * MOBA_BLOCK
        logits = jnp.concatenate([l_sel.reshape(Bt, H, MOBA_QCHUNK, n_sel), l_own], axis=-1)
        p = jax.nn.softmax(logits, axis=-1)
        p_sel = p[..., :n_sel].reshape(Bt, H, MOBA_QCHUNK, topk, MOBA_BLOCK)
        return (jnp.einsum('bhqtk,bhqtkd->bhqd', p_sel, vsel)
                + jnp.einsum('bhqk,bhkd->bhqd', p[..., n_sel:], vown))

    out = lax.map(one_chunk, (q_chunks, jnp.arange(nqc, dtype=jnp.int32)))
    return out.transpose(1, 2, 0, 3, 4).reshape(Bt, H, S, Dh)


def dilated_attention(q, k, v, bias_table):
    Bt, H, S, Dh = q.shape
    scale = Dh ** -0.5
    i = jnp.arange(DIL_BLOCK)[:, None]
    j = jnp.arange(2 * DIL_BLOCK)[None, :]
    dist_sub = DIL_BLOCK + i - j
    band = (dist_sub >= 0) & (dist_sub <= DIL_BLOCK)
    outs, lses = [], []
    for window, dil in DIL_PAIRS:
        L = S // dil
        nb = -(-L // DIL_BLOCK)
        Lp = nb * DIL_BLOCK

        def to_sub(t):
            t = t.reshape(Bt, H, L, dil, Dh).transpose(0, 1, 3, 2, 4)
            return jnp.pad(t, ((0, 0), (0, 0), (0, 0), (0, Lp - L), (0, 0)))

        def band_keys(t):
            t = jnp.pad(t, ((0, 0), (0, 0), (0, 0), (DIL_BLOCK, 0), (0, 0)))
            t = t.reshape(Bt, H, dil, nb + 1, DIL_BLOCK, Dh)
            return jnp.concatenate([t[:, :, :, :-1], t[:, :, :, 1:]], axis=4)

        qb = to_sub(q).reshape(Bt, H, dil, nb, DIL_BLOCK, Dh).astype(jnp.float32)
        kb = band_keys(to_sub(k)).astype(jnp.float32)
        vb = band_keys(to_sub(v)).astype(jnp.float32)
        logits = jnp.einsum('bhrnqd,bhrnkd->bhrnqk', qb, kb) * scale
        bias = bias_table[rel_bucket(jnp.maximum(dist_sub, 0) * dil)].astype(jnp.float32)
        logits = logits + bias.transpose(2, 0, 1)[None, :, None, None]
        key_pos = jnp.arange(nb)[:, None, None] * DIL_BLOCK + j[None] - DIL_BLOCK
        mask = band[None] & (key_pos >= 0)
        logits = jnp.where(mask, logits, NEG)
        m = logits.max(-1, keepdims=True)
        p = jnp.exp(logits - m)
        s = p.sum(-1, keepdims=True)
        o = jnp.einsum('bhrnqk,bhrnkd->bhrnqd', p, vb) / s
        lse = (m + jnp.log(s))[..., 0]
        o = o.reshape(Bt, H, dil, Lp, Dh)[:, :, :, :L].transpose(0, 1, 3, 2, 4).reshape(Bt, H, S, Dh)
        lse = lse.reshape(Bt, H, dil, Lp)[:, :, :, :L].transpose(0, 1, 3, 2).reshape(Bt, H, S)
        outs.append(o)
        lses.append(lse)
    w = jax.nn.softmax(jnp.stack(lses, 0), axis=0)
    return jnp.einsum('pbhs,pbhsd->bhsd', w, jnp.stack(outs, 0))


def s5_mixer(u, a_re, a_im, log_dt, b_re, b_im, c_re, c_im, d_skip, glu_w, glu_b):
    Bt, S, _ = u.shape
    uf = u.astype(jnp.float32)
    ug = uf.reshape(Bt, S, S5_GROUPS, S5_GROUP)
    ar = a_re.astype(jnp.float32)
    ai = a_im.astype(jnp.float32)
    dt = jnp.exp(log_dt.astype(jnp.float32))[:, None]
    mag = jnp.exp(dt * ar)
    abar_re = mag * jnp.cos(dt * ai)
    abar_im = mag * jnp.sin(dt * ai)
    nr, ni = abar_re - 1.0, abar_im
    den = ar * ar + ai * ai
    zr = (nr * ar + ni * ai) / den
    zi = (ni * ar - nr * ai) / den
    br = b_re.astype(jnp.float32)
    bi = b_im.astype(jnp.float32)
    bbar_re = zr[..., None] * br - zi[..., None] * bi
    bbar_im = zr[..., None] * bi + zi[..., None] * br
    xr = jnp.einsum('bsgc,gpc->bsgp', ug, bbar_re)
    xi = jnp.einsum('bsgc,gpc->bsgp', ug, bbar_im)
    at_re = jnp.broadcast_to(abar_re, xr.shape)
    at_im = jnp.broadcast_to(abar_im, xr.shape)

    def combine(e1, e2):
        a1r, a1i, b1r, b1i = e1
        a2r, a2i, b2r, b2i = e2
        return (a2r * a1r - a2i * a1i, a2r * a1i + a2i * a1r,
                a2r * b1r - a2i * b1i + b2r, a2r * b1i + a2i * b1r + b2i)

    _, _, hr, hi = lax.associative_scan(combine, (at_re, at_im, xr, xi), axis=1)
    y = (jnp.einsum('bsgp,gcp->bsgc', hr, c_re.astype(jnp.float32))
         - jnp.einsum('bsgp,gcp->bsgc', hi, c_im.astype(jnp.float32)))
    y = y.reshape(Bt, S, BRANCH_W) + d_skip.astype(jnp.float32) * uf
    y = jax.nn.gelu(y)
    return y * jax.nn.sigmoid(y @ glu_w.astype(jnp.float32) + glu_b.astype(jnp.float32))


def hgrn2_mixer(q, f_pre, i_in, lower_bound):
    Bt, S, _ = q.shape
    C = HGRN_CHUNK
    nc = S // C
    lb = lower_bound.astype(jnp.float32)
    f = lb + (1.0 - lb) * jax.nn.sigmoid(f_pre.astype(jnp.float32))
    log_f = jnp.log(f)
    k = 1.0 - f
    qf = jax.nn.silu(q.astype(jnp.float32)) * HEAD_DIM ** -0.5

    def chunks(t):
        return t.reshape(Bt, nc, C, N_HEADS_BR, HEAD_DIM).transpose(1, 0, 3, 2, 4)

    causal = jnp.tril(jnp.ones((C, C), dtype=bool))

    def step(state, inp):
        qc, kc, vc, gc = inp
        b = jnp.cumsum(gc, axis=2)
        diff = b[:, :, :, None, :] - b[:, :, None, :, :]
        decay = jnp.exp(jnp.where(causal[:, :, None], diff, NEG))
        scores = jnp.einsum('bhtd,bhsd,bhtsd->bhts', qc, kc, decay)
        o = (jnp.einsum('bhts,bhsv->bhtv', scores, vc)
             + jnp.einsum('bhtd,bhdv->bhtv', qc * jnp.exp(b), state))
        b_last = b[:, :, -1:]
        new_state = (jnp.exp(b_last[:, :, 0])[..., None] * state
                     + jnp.einsum('bhsd,bhsv->bhdv', kc * jnp.exp(b_last - b), vc))
        return new_state, o

    state0 = jnp.zeros((Bt, N_HEADS_BR, HEAD_DIM, HEAD_DIM), jnp.float32)
    _, o = lax.scan(step, state0, (chunks(qf), chunks(k), chunks(i_in.astype(jnp.float32)), chunks(log_f)))
    return o.transpose(1, 0, 3, 2, 4).reshape(Bt, S, BRANCH_W)


def setup_inputs(seed: int = 0) -> dict:
    key = jax.random.key(seed)
    ks = jax.random.split(key, 20)
    D, G, P, C = D_MODEL, S5_GROUPS, S5_STATE, S5_GROUP

    def nrm(k, shape, s):
        return jax.random.normal(k, shape, jnp.float32) * s

    n = jnp.arange(P, dtype=jnp.float32)
    return {
        'x': nrm(ks[0], (BATCH, SEQ, D), 1.0),
        'w_in': nrm(ks[1], (DEPTH, D, IN_COLS), D ** -0.5),
        'rel_bias': nrm(ks[2], (REL_BUCKETS, N_ATTN_HEADS), 0.2),
        's5_a_re': -0.5 + nrm(ks[3], (DEPTH, G, P), 0.01),
        's5_a_im': jnp.pi * n + nrm(ks[4], (DEPTH, G, P), 0.01),
        's5_log_dt': jax.random.uniform(ks[5], (DEPTH, G), jnp.float32, math.log(1e-3), math.log(1e-1)),
        's5_b_re': nrm(ks[6], (DEPTH, G, P, C), (2 * C) ** -0.5),
        's5_b_im': nrm(ks[7], (DEPTH, G, P, C), (2 * C) ** -0.5),
        's5_c_re': nrm(ks[8], (DEPTH, G, C, P), P ** -0.5),
        's5_c_im': nrm(ks[9], (DEPTH, G, C, P), P ** -0.5),
        's5_d': nrm(ks[10], (DEPTH, BRANCH_W), 0.5),
        's5_glu_w': nrm(ks[11], (DEPTH, BRANCH_W, BRANCH_W), BRANCH_W ** -0.5),
        's5_glu_b': nrm(ks[12], (DEPTH, BRANCH_W), 0.01),
        'hgrn_lower': nrm(ks[13], (DEPTH, BRANCH_W), 0.1),
        'branch_gain': 1.0 + nrm(ks[14], (DEPTH, D), 0.02),
        'w_out': nrm(ks[15], (DEPTH, D, D), D ** -0.5 * BETA),
        'ln_g': 1.0 + nrm(ks[16], (DEPTH, D), 0.02),
        'ln_b': nrm(ks[17], (DEPTH, D), 0.02),
    }


def reference(x, w_in, rel_bias, s5_a_re, s5_a_im, s5_log_dt, s5_b_re, s5_b_im, s5_c_re, s5_c_im,
              s5_d, s5_glu_w, s5_glu_b, hgrn_lower, branch_gain, w_out, ln_g, ln_b):
    Bt, S, D = x.shape
    W = BRANCH_W
    p_lb = jax.nn.softmax(hgrn_lower.astype(jnp.float32), axis=0)
    lb_all = jnp.cumsum(p_lb, axis=0) - p_lb[0]

    def heads(t):
        return t.reshape(Bt, S, N_HEADS_BR, HEAD_DIM).transpose(0, 2, 1, 3)

    def merge(t):
        return t.transpose(0, 2, 1, 3).reshape(Bt, S, W)

    for l in range(DEPTH):
        proj = x @ w_in[l]
        qa, ka, va, us, qc, fc, ic, qd, kd, vd = [proj[..., s * W:(s + 1) * W] for s in range(N_IN_SLOTS)]
        gates = proj[..., N_IN_SLOTS * W:]
        ya = merge(moba_attention(heads(qa), heads(ka), heads(va), rel_bias[:, :N_HEADS_BR]))
        yb = s5_mixer(us, s5_a_re[l], s5_a_im[l], s5_log_dt[l], s5_b_re[l], s5_b_im[l],
                      s5_c_re[l], s5_c_im[l], s5_d[l], s5_glu_w[l], s5_glu_b[l])
        yc = hgrn2_mixer(qc, fc, ic, lb_all[l])
        yd = merge(dilated_attention(heads(qd), heads(kd), heads(vd), rel_bias[:, N_HEADS_BR:]))
        y = jnp.concatenate([ya, yb, yc, yd], axis=-1).astype(jnp.float32)
        yg = y.reshape(Bt, S, D // HEAD_DIM, HEAD_DIM)
        yg = yg * lax.rsqrt(jnp.mean(yg * yg, axis=-1, keepdims=True) + RMS_EPS)
        y = (yg.reshape(Bt, S, D) * branch_gain[l].astype(jnp.float32)
             * jax.nn.silu(gates.astype(jnp.float32)))
        y = y.astype(x.dtype) @ w_out[l]
        x = layer_norm(ALPHA * x + y, ln_g[l], ln_b[l]).astype(x.dtype)
    return x
```

```python
import functools
import math

import numpy as np
import jax
import jax.numpy as jnp
from jax import lax
from jax.experimental import pallas as pl
from jax.experimental.pallas import tpu as pltpu

F32 = jnp.float32
BF16 = jnp.bfloat16

HEAD_DIM = 64
N_HEADS_BR = 4
BRANCH_W = N_HEADS_BR * HEAD_DIM
N_IN_SLOTS = 10
MOBA_BLOCK = 256
MOBA_TOPK = 3
S5_GROUP = 16
S5_GROUPS = BRANCH_W // S5_GROUP
S5_STATE = 64
S5_LANES = S5_GROUPS * S5_STATE
DIL_PAIRS = ((128, 1), (512, 4), (2048, 16))
DIL_BLOCK = 128
REL_BUCKETS = 32
REL_MAX_DIST = 2048
LN_EPS = 1e-5
RMS_EPS = 1e-6
NEG = -1e30
HGRN_SUB = 16
SUBLANES = 8
V7X_VMEM_LIMIT = 56 * 1024 * 1024


def _bucket_thresholds():
    max_exact = REL_BUCKETS // 2
    d = np.arange(0, 4 * REL_MAX_DIST, dtype=np.int64)
    large = max_exact + (np.log(np.maximum(d, 1) / max_exact) / math.log(REL_MAX_DIST / max_exact)
                         * (REL_BUCKETS - max_exact)).astype(np.int64)
    bucket = np.where(d < max_exact, d, np.minimum(large, REL_BUCKETS - 1))
    assert np.all(np.diff(bucket) >= 0)
    return [int(np.argmax(bucket >= j)) for j in range(REL_BUCKETS)]


_BUCKET_THR = _bucket_thresholds()


def _bias_from_dist(dist, tab_ref, head):
    val = jnp.full(dist.shape, tab_ref[0, head], F32)
    for j in range(1, REL_BUCKETS):
        val = jnp.where(dist >= _BUCKET_THR[j], tab_ref[j, head], val)
    return val


def _dot_nt(a, b):
    return lax.dot_general(a, b, (((1,), (1,)), ((), ())), preferred_element_type=F32)


def _dot_tn(a, b):
    return lax.dot_general(a, b, (((0,), (0,)), ((), ())), preferred_element_type=F32)


def _dot(a, b):
    return jnp.dot(a, b, preferred_element_type=F32)


def _lane_head(shape):
    return lax.shift_right_logical(lax.broadcasted_iota(jnp.int32, shape, 1), 6)


def _stack_heads(q):
    lh = _lane_head(q.shape)
    zero = jnp.zeros_like(q)
    return jnp.concatenate([jnp.where(lh == h, q, zero) for h in range(N_HEADS_BR)], axis=0)


def _merge_heads(o4, n):
    lh = _lane_head((n, BRANCH_W))
    out = jnp.zeros((n, BRANCH_W), F32)
    for h in range(N_HEADS_BR):
        out = jnp.where(lh == h, o4[h * n:(h + 1) * n, :], out)
    return out


def _params(*sem):
    return pltpu.CompilerParams(dimension_semantics=sem, vmem_limit_bytes=V7X_VMEM_LIMIT)


_BF16_SLOTS = (0, 1, 2, 7, 8, 9)


def _inproj_kernel(x_ref, w_ref, *out_refs):
    xb = x_ref[...].astype(BF16)
    for s in range(N_IN_SLOTS):
        r = _dot(xb, w_ref[:, s * BRANCH_W:(s + 1) * BRANCH_W])
        out_refs[s][...] = r.astype(out_refs[s].dtype)
    g0 = N_IN_SLOTS * BRANCH_W
    out_refs[N_IN_SLOTS][...] = _dot(xb, w_ref[:, g0:])


def _inproj(xf, w):
    t, d = xf.shape
    tm = 512
    ncol = w.shape[1]
    shapes = [jax.ShapeDtypeStruct((t, BRANCH_W), BF16 if s in _BF16_SLOTS else F32)
              for s in range(N_IN_SLOTS)]
    shapes.append(jax.ShapeDtypeStruct((t, ncol - N_IN_SLOTS * BRANCH_W), F32))
    out_specs = [pl.BlockSpec((tm, sh.shape[1]), lambda i: (i, 0)) for sh in shapes]
    return pl.pallas_call(
        _inproj_kernel,
        grid=(t // tm,),
        in_specs=[pl.BlockSpec((tm, d), lambda i: (i, 0)),
                  pl.BlockSpec((d, ncol), lambda i: (0, 0))],
        out_specs=out_specs,
        out_shape=shapes,
        compiler_params=_params("arbitrary"),
        name="inproj",
    )(xf, w)


_MOBA_BIAS_TILES = REL_MAX_DIST // MOBA_BLOCK + 2


def _moba_kernel(tab_ref, q_ref, k_ref, v_ref, o_ref, bias_sc, kmean_sc, acc_sc, m_sc, l_sc):
    b = pl.program_id(0)
    i = pl.program_id(1)
    blk = MOBA_BLOCK
    nkb = k_ref.shape[0] // blk
    nh = N_HEADS_BR

    @pl.when((b == 0) & (i == 0))
    def _():
        row = lax.broadcasted_iota(jnp.int32, (blk, blk), 0)
        col = lax.broadcasted_iota(jnp.int32, (blk, blk), 1)
        for m in range(_MOBA_BIAS_TILES):
            for h in range(nh):
                if m == _MOBA_BIAS_TILES - 1:
                    tile = jnp.full((blk, blk), tab_ref[REL_BUCKETS - 1, h], F32)
                else:
                    dist = jnp.maximum(m * blk + row - col, 0)
                    tile = _bias_from_dist(dist, tab_ref, h)
                if m == 0:
                    tile = jnp.where(row >= col, tile, NEG)
                bias_sc[m, h * blk:(h + 1) * blk, :] = tile

    @pl.when(i == 0)
    def _():
        for n in range(nkb):
            kb = k_ref[n * blk:(n + 1) * blk, :].astype(F32)
            kmean_sc[n:n + 1, :] = jnp.sum(kb, axis=0, keepdims=True) * (1.0 / blk)

    q4 = _stack_heads(q_ref[...] * jnp.asarray(HEAD_DIM ** -0.5, BF16))

    km = kmean_sc[...]
    km_hi = km.astype(BF16)
    km_lo = (km - km_hi.astype(F32)).astype(BF16)
    gate = _dot_nt(q4, km_hi) + _dot_nt(q4, km_lo)
    col = lax.broadcasted_iota(jnp.int32, gate.shape, 1)
    gate = jnp.where(col < i, gate, -jnp.inf)
    sel = jnp.zeros(gate.shape, F32)
    for _ in range(MOBA_TOPK):
        mx = jnp.max(gate, axis=-1, keepdims=True)
        idx = jnp.min(jnp.where(gate == mx, col, nkb), axis=-1, keepdims=True)
        pick = (col == idx) & (mx > -jnp.inf)
        sel = jnp.where(pick, 1.0, sel)
        gate = jnp.where(pick, -jnp.inf, gate)

    r0 = pl.multiple_of(i * blk, blk)
    s = _dot_nt(q4, k_ref[pl.ds(r0, blk), :]) + bias_sc[0]
    m0 = jnp.max(s, axis=-1, keepdims=True)
    p = jnp.exp(s - m0)
    m_sc[...] = m0
    l_sc[...] = jnp.sum(p, axis=-1, keepdims=True)
    acc_sc[...] = _dot(p.astype(BF16), v_ref[pl.ds(r0, blk), :])

    def body(n, carry):
        k0 = pl.multiple_of(n * blk, blk)
        bias = bias_sc[jnp.minimum(i - n, _MOBA_BIAS_TILES - 1)]
        chosen = jnp.sum(jnp.where(col == n, sel, 0.0), axis=-1, keepdims=True) > 0.0
        s = _dot_nt(q4, k_ref[pl.ds(k0, blk), :]) + bias
        s = jnp.where(chosen, s, NEG)
        m_old = m_sc[...]
        m_new = jnp.maximum(m_old, jnp.max(s, axis=-1, keepdims=True))
        alpha = jnp.exp(m_old - m_new)
        p = jnp.exp(s - m_new)
        l_sc[...] = alpha * l_sc[...] + jnp.sum(p, axis=-1, keepdims=True)
        acc_sc[...] = alpha * acc_sc[...] + _dot(p.astype(BF16), v_ref[pl.ds(k0, blk), :])
        m_sc[...] = m_new
        return carry

    lax.fori_loop(0, i, body, 0)
    o_ref[...] = _merge_heads(acc_sc[...] / l_sc[...], blk)


def _moba(tab, q, k, v):
    bsz, s, w = q.shape
    blk = MOBA_BLOCK
    nkb = s // blk
    return pl.pallas_call(
        _moba_kernel,
        grid=(bsz, nkb),
        in_specs=[pl.BlockSpec(memory_space=pltpu.SMEM),
                  pl.BlockSpec((None, blk, w), lambda b, i: (b, i, 0)),
                  pl.BlockSpec((None, s, w), lambda b, i: (b, 0, 0)),
                  pl.BlockSpec((None, s, w), lambda b, i: (b, 0, 0))],
        out_specs=pl.BlockSpec((None, blk, w), lambda b, i: (b, i, 0)),
        out_shape=jax.ShapeDtypeStruct((bsz, s, w), F32),
        scratch_shapes=[pltpu.VMEM((_MOBA_BIAS_TILES, N_HEADS_BR * blk, blk), F32),
                        pltpu.VMEM((nkb, w), F32),
                        pltpu.VMEM((N_HEADS_BR * blk, w), F32),
                        pltpu.VMEM((N_HEADS_BR * blk, 1), F32),
                        pltpu.VMEM((N_HEADS_BR * blk, 1), F32)],
        compiler_params=_params("arbitrary", "arbitrary"),
        name="moba",
    )(tab, q, k, v)


def _dil_kernel(tab_ref, q_ref, kc_ref, kp_ref, vc_ref, vp_ref, o_ref, lse_ref, bias_sc, *, dil, tq):
    nb = DIL_BLOCK
    nh = N_HEADS_BR
    t = pl.program_id(2)
    first = (pl.program_id(0) == 0) & (pl.program_id(1) == 0) & (t == 0)

    @pl.when(first)
    def _():
        row = lax.broadcasted_iota(jnp.int32, (nb, 2 * nb), 0)
        col = lax.broadcasted_iota(jnp.int32, (nb, 2 * nb), 1)
        dist_sub = nb + row - col
        band = (dist_sub >= 0) & (dist_sub <= nb)
        for h in range(nh):
            tile = _bias_from_dist(jnp.maximum(dist_sub, 0) * dil, tab_ref, nh + h)
            bias_sc[h * nb:(h + 1) * nb, :] = jnp.where(band, tile, NEG)

    scale = jnp.asarray(HEAD_DIM ** -0.5, BF16)
    for jb in range(tq // nb):
        lo, hi = jb * nb, (jb + 1) * nb
        q4 = _stack_heads(q_ref[lo:hi, :] * scale)
        if jb == 0:
            kprev, vprev = kp_ref[tq - nb:tq, :], vp_ref[tq - nb:tq, :]
        else:
            kprev, vprev = kc_ref[lo - nb:lo, :], vc_ref[lo - nb:lo, :]
        keys = jnp.concatenate([kprev, kc_ref[lo:hi, :]], axis=0)
        vals = jnp.concatenate([vprev, vc_ref[lo:hi, :]], axis=0)
        s = _dot_nt(q4, keys) + bias_sc[...]
        if jb == 0:
            col = lax.broadcasted_iota(jnp.int32, s.shape, 1)
            s = jnp.where((col >= nb) | (t > 0), s, NEG)
        m = jnp.max(s, axis=-1, keepdims=True)
        p = jnp.exp(s - m)
        l = jnp.sum(p, axis=-1, keepdims=True)
        o4 = _dot(p.astype(BF16), vals) / l
        lse4 = jnp.broadcast_to(m + jnp.log(l), o4.shape)
        o_ref[lo:hi, :] = _merge_heads(o4, nb)
        lse_ref[lo:hi, :] = _merge_heads(lse4, nb)


def _dilated(tab, q, k, v, dil):
    bsz, s, w = q.shape
    sub_len = s // dil
    tq = min(512, sub_len)
    nt = sub_len // tq
    view = lambda a: a.reshape(bsz, sub_len, dil * w)
    cur = pl.BlockSpec((None, tq, w), lambda b, r, t: (b, t, r))
    prev = pl.BlockSpec((None, tq, w), lambda b, r, t: (b, jnp.maximum(t - 1, 0), r))
    out_shape = jax.ShapeDtypeStruct((bsz, sub_len, dil * w), F32)
    o, lse = pl.pallas_call(
        functools.partial(_dil_kernel, dil=dil, tq=tq),
        grid=(bsz, dil, nt),
        in_specs=[pl.BlockSpec(memory_space=pltpu.SMEM), cur, cur, prev, cur, prev],
        out_specs=[cur, cur],
        out_shape=[out_shape, out_shape],
        scratch_shapes=[pltpu.VMEM((N_HEADS_BR * DIL_BLOCK, 2 * DIL_BLOCK), F32)],
        compiler_params=_params("arbitrary", "arbitrary", "arbitrary"),
        name=f"dilated{dil}",
    )(tab, view(q), view(k), view(k), view(v), view(v))
    return o.reshape(bsz, s, w), lse.reshape(bsz, s, w)


_S5_STEPS = (1, 2, 4)


def _s5_kernel(u_ref, wb_ref, coef_ref, wc_ref, d_ref, gw_ref, gb_ref, o_ref, h_sc, carry_sc):
    ts = u_ref.shape[0]
    nl = S5_LANES

    @pl.when(pl.program_id(1) == 0)
    def _():
        carry_sc[...] = jnp.zeros_like(carry_sc)

    u = u_ref[...]
    h_sc[...] = _dot(u.astype(BF16), wb_ref[...])

    def cfma(re, im, cr, ci, sre, sim):
        return re + cr * sre - ci * sim, im + cr * sim + ci * sre

    def body(g, carry):
        cre, cim = carry
        r0 = pl.multiple_of(g * SUBLANES, SUBLANES)
        re = h_sc[pl.ds(r0, SUBLANES), :nl]
        im = h_sc[pl.ds(r0, SUBLANES), nl:]
        for si, k in enumerate(_S5_STEPS):
            re, im = cfma(re, im, coef_ref[si, 0], coef_ref[si, 1],
                          pltpu.roll(re, k, axis=0), pltpu.roll(im, k, axis=0))
        last = len(_S5_STEPS)
        re, im = cfma(re, im, coef_ref[last, 0], coef_ref[last, 1],
                      jnp.broadcast_to(cre, re.shape), jnp.broadcast_to(cim, im.shape))
        h_sc[pl.ds(r0, SUBLANES), :nl] = re
        h_sc[pl.ds(r0, SUBLANES), nl:] = im
        return re[SUBLANES - 1:, :], im[SUBLANES - 1:, :]

    cre, cim = lax.fori_loop(0, ts // SUBLANES, body, (carry_sc[0:1, :nl], carry_sc[0:1, nl:]))
    carry_sc[0:1, :nl] = cre
    carry_sc[0:1, nl:] = cim

    y = _dot(h_sc[...].astype(BF16), wc_ref[...]) + d_ref[...] * u
    y = 0.5 * y * (1.0 + jnp.tanh(math.sqrt(2.0 / math.pi) * (y + 0.044715 * (y * y * y))))
    z = _dot(y.astype(BF16), gw_ref[...]) + gb_ref[...]
    o_ref[...] = y * jax.nn.sigmoid(z)


def _s5_weights(a_re, a_im, log_dt, b_re, b_im, c_re, c_im):
    g, p, c = b_re.shape
    ar, ai = a_re.astype(F32), a_im.astype(F32)
    dt = jnp.exp(log_dt.astype(F32))[:, None]

    def abar_pow(k):
        mag = jnp.exp(k * dt * ar)
        return (mag * jnp.cos(k * dt * ai)).reshape(-1), (mag * jnp.sin(k * dt * ai)).reshape(-1)

    abar_re, abar_im = jnp.exp(dt * ar) * jnp.cos(dt * ai), jnp.exp(dt * ar) * jnp.sin(dt * ai)
    nr, ni = abar_re - 1.0, abar_im
    den = ar * ar + ai * ai
    zr = (nr * ar + ni * ai) / den
    zi = (ni * ar - nr * ai) / den
    br, bi = b_re.astype(F32), b_im.astype(F32)
    bbar_re = zr[..., None] * br - zi[..., None] * bi
    bbar_im = zr[..., None] * bi + zi[..., None] * br
    eye = jnp.eye(g, dtype=F32)
    blockdiag_in = lambda m: jnp.einsum('gpc,gh->gchp', m, eye).reshape(g * c, g * p)
    wb = jnp.concatenate([blockdiag_in(bbar_re), blockdiag_in(bbar_im)], axis=1)
    blockdiag_out = lambda m: jnp.einsum('gcp,gh->gphc', m, eye).reshape(g * p, g * c)
    wc = jnp.concatenate([blockdiag_out(c_re.astype(F32)), -blockdiag_out(c_im.astype(F32))], axis=0)
    rows = jnp.arange(SUBLANES)[:, None]
    coefs = []
    for k in _S5_STEPS:
        pr, pi = abar_pow(k)
        coefs.append(jnp.stack([jnp.where(rows >= k, pr[None, :], 0.0),
                                jnp.where(rows >= k, pi[None, :], 0.0)]))
    per_row = [abar_pow(k + 1) for k in range(SUBLANES)]
    coefs.append(jnp.stack([jnp.stack([r for r, _ in per_row]), jnp.stack([i for _, i in per_row])]))
    return wb.astype(BF16), jnp.stack(coefs), wc.astype(BF16)


def _s5(u, wb, coef, wc, d_skip, glu_w, glu_b):
    bsz, s, w = u.shape
    ts = min(512, s)
    const = lambda shape: pl.BlockSpec(shape, lambda b, t: (0,) * len(shape))
    return pl.pallas_call(
        _s5_kernel,
        grid=(bsz, s // ts),
        in_specs=[pl.BlockSpec((None, ts, w), lambda b, t: (b, t, 0)),
                  const(wb.shape), const(coef.shape), const(wc.shape),
                  const((1, w)), const(glu_w.shape), const((1, w))],
        out_specs=pl.BlockSpec((None, ts, w), lambda b, t: (b, t, 0)),
        out_shape=jax.ShapeDtypeStruct((bsz, s, w), F32),
        scratch_shapes=[pltpu.VMEM((ts, 2 * S5_LANES), F32),
                        pltpu.VMEM((SUBLANES, 2 * S5_LANES), F32)],
        compiler_params=_params("arbitrary", "arbitrary"),
        name="s5",
    )(u, wb, coef, wc, d_skip.reshape(1, w), glu_w, glu_b.reshape(1, w))


def _hgrn_kernel(lb_ref, q_ref, f_ref, i_ref, o_ref, st_sc):
    th = q_ref.shape[0]
    c = HGRN_SUB
    w = BRANCH_W

    @pl.when(pl.program_id(1) == 0)
    def _():
        st_sc[...] = jnp.zeros_like(st_sc)

    lb = lb_ref[...]
    rowi = lax.broadcasted_iota(jnp.int32, (c, w), 0)
    same_head = (lax.shift_right_logical(lax.broadcasted_iota(jnp.int32, (w, w), 0), 6)
                 == lax.shift_right_logical(lax.broadcasted_iota(jnp.int32, (w, w), 1), 6))
    head_ones = jnp.where(same_head, 1.0, 0.0).astype(BF16)

    def body(ci, carry):
        r0 = pl.multiple_of(ci * c, c)
        q = q_ref[pl.ds(r0, c), :]
        qf = q * jax.nn.sigmoid(q) * (HEAD_DIM ** -0.5)
        f = lb + (1.0 - lb) * jax.nn.sigmoid(f_ref[pl.ds(r0, c), :])
        kk = 1.0 - f
        v = i_ref[pl.ds(r0, c), :]
        b = jnp.log(f)
        k = 1
        while k < c:
            b = b + jnp.where(rowi >= k, pltpu.roll(b, k, axis=0), 0.0)
            k *= 2
        e = jnp.concatenate(
            [jnp.exp(jnp.where(rowi >= s, b - b[s:s + 1, :], NEG)) * (qf * kk[s:s + 1, :])
             for s in range(c)], axis=0)
        wgt = _dot(e.astype(BF16), head_ones)
        o = jnp.zeros((c, w), F32)
        for s in range(c):
            o = o + wgt[s * c:(s + 1) * c, :] * v[s:s + 1, :]
        st = st_sc[...]
        o = o + _dot_nt((qf * jnp.exp(b)).astype(BF16), st.astype(BF16))
        b_last = b[c - 1:c, :]
        upd = _dot_tn(v.astype(BF16), (kk * jnp.exp(b_last - b)).astype(BF16))
        st_sc[...] = jnp.exp(b_last) * st + jnp.where(same_head, upd, 0.0)
        o_ref[pl.ds(r0, c), :] = o
        return carry

    lax.fori_loop(0, th // c, body, 0)


def _hgrn(lb, q, f, i):
    bsz, s, w = q.shape
    th = min(512, s)
    tile = pl.BlockSpec((None, th, w), lambda b, t: (b, t, 0))
    return pl.pallas_call(
        _hgrn_kernel,
        grid=(bsz, s // th),
        in_specs=[pl.BlockSpec((1, w), lambda b, t: (0, 0)), tile, tile, tile],
        out_specs=tile,
        out_shape=jax.ShapeDtypeStruct((bsz, s, w), F32),
        scratch_shapes=[pltpu.VMEM((w, w), F32)],
        compiler_params=_params("arbitrary", "arbitrary"),
        name="hgrn2",
    )(lb.reshape(1, w), q, f, i)


def _outproj_kernel(ya_ref, yb_ref, yc_ref, o1_ref, o2_ref, o3_ref, l1_ref, l2_ref, l3_ref,
                    gates_ref, x_ref, w_ref, gain_ref, lng_ref, lnb_ref, out_ref, *, alpha):
    l1, l2, l3 = l1_ref[...], l2_ref[...], l3_ref[...]
    lm = jnp.maximum(jnp.maximum(l1, l2), l3)
    e1, e2, e3 = jnp.exp(l1 - lm), jnp.exp(l2 - lm), jnp.exp(l3 - lm)
    yd = (e1 * o1_ref[...] + e2 * o2_ref[...] + e3 * o3_ref[...]) / (e1 + e2 + e3)

    w = BRANCH_W
    same_head = (lax.shift_right_logical(lax.broadcasted_iota(jnp.int32, (w, w), 0), 6)
                 == lax.shift_right_logical(lax.broadcasted_iota(jnp.int32, (w, w), 1), 6))
    head_ones = jnp.where(same_head, 1.0, 0.0).astype(BF16)
    normed = []
    for y in (ya_ref[...], yb_ref[...], yc_ref[...], yd):
        y2 = y * y
        hi = y2.astype(BF16)
        lo = (y2 - hi.astype(F32)).astype(BF16)
        ss = _dot(hi, head_ones) + _dot(lo, head_ones)
        normed.append(y * lax.rsqrt(ss * (1.0 / HEAD_DIM) + RMS_EPS))
    g = gates_ref[...]
    y = jnp.concatenate(normed, axis=-1) * gain_ref[...] * (g * jax.nn.sigmoid(g))
    r = alpha * x_ref[...] + _dot(y.astype(BF16), w_ref[...])
    mu = jnp.mean(r, axis=-1, keepdims=True)
    rc = r - mu
    var = jnp.mean(rc * rc, axis=-1, keepdims=True)
    out_ref[...] = rc * lax.rsqrt(var + LN_EPS) * lng_ref[...] + lnb_ref[...]


def _outproj(branches, gates, xf, w, gain, ln_g, ln_b, alpha):
    t, d = xf.shape
    tm = 256
    wide = pl.BlockSpec((tm, d), lambda i: (i, 0))
    narrow = pl.BlockSpec((tm, BRANCH_W), lambda i: (i, 0))
    vec = pl.BlockSpec((1, d), lambda i: (0, 0))
    return pl.pallas_call(
        functools.partial(_outproj_kernel, alpha=alpha),
        grid=(t // tm,),
        in_specs=[narrow] * 9 + [wide, wide, pl.BlockSpec((d, d), lambda i: (0, 0)), vec, vec, vec],
        out_specs=wide,
        out_shape=jax.ShapeDtypeStruct((t, d), F32),
        compiler_params=_params("arbitrary"),
        name="outproj",
    )(*branches, gates, xf, w, gain.reshape(1, d), ln_g.reshape(1, d), ln_b.reshape(1, d))


def kernel(x, w_in, rel_bias, s5_a_re, s5_a_im, s5_log_dt, s5_b_re, s5_b_im, s5_c_re, s5_c_im,
           s5_d, s5_glu_w, s5_glu_b, hgrn_lower, branch_gain, w_out, ln_g, ln_b):
    bsz, s, d = x.shape
    depth = w_in.shape[0]
    t = bsz * s
    w = BRANCH_W
    alpha = (2 * depth) ** 0.25
    p_lb = jax.nn.softmax(hgrn_lower.astype(F32), axis=0)
    lb_all = jnp.cumsum(p_lb, axis=0) - p_lb[0]
    tab = rel_bias.astype(F32)
    w_in_b = w_in.astype(BF16)
    w_out_b = w_out.astype(BF16)
    glu_w_b = s5_glu_w.astype(BF16)

    xf = x.reshape(t, d).astype(F32)
    for l in range(depth):
        qa, ka, va, us, qc, fc, ic, qd, kd, vd, gates = _inproj(xf, w_in_b[l])
        seq = lambda a: a.reshape(bsz, s, w)
        ya = _moba(tab, seq(qa), seq(ka), seq(va))
        wb, coef, wc = _s5_weights(s5_a_re[l], s5_a_im[l], s5_log_dt[l], s5_b_re[l], s5_b_im[l],
                                   s5_c_re[l], s5_c_im[l])
        yb = _s5(seq(us), wb, coef, wc, s5_d[l].astype(F32), glu_w_b[l], s5_glu_b[l].astype(F32))
        yc = _hgrn(lb_all[l], seq(qc), seq(fc), seq(ic))
        dil_out = [_dilated(tab, seq(qd), seq(kd), seq(vd), dil) for _, dil in DIL_PAIRS]
        flat = lambda a: a.reshape(t, w)
        branches = ([flat(ya), flat(yb), flat(yc)] + [flat(o) for o, _ in dil_out]
                    + [flat(lse) for _, lse in dil_out])
        xf = _outproj(branches, gates, xf, w_out_b[l], branch_gain[l].astype(F32),
                      ln_g[l].astype(F32), ln_b[l].astype(F32), alpha)
    return xf.reshape(bsz, s, d).astype(x.dtype)
```

```python
import functools
import math

import numpy as np
import jax
import jax.numpy as jnp
from jax import lax
from jax.experimental import pallas as pl
from jax.experimental.pallas import tpu as pltpu

F32 = jnp.float32
BF16 = jnp.bfloat16

HEAD_DIM = 64
N_HEADS_BR = 4
BRANCH_W = N_HEADS_BR * HEAD_DIM
N_IN_SLOTS = 10
MOBA_BLOCK = 256
MOBA_TOPK = 3
S5_GROUP = 16
S5_GROUPS = BRANCH_W // S5_GROUP
S5_STATE = 64
S5_LANES = S5_GROUPS * S5_STATE
DIL_PAIRS = ((128, 1), (512, 4), (2048, 16))
DIL_BLOCK = 128
REL_BUCKETS = 32
REL_MAX_DIST = 2048
LN_EPS = 1e-5
RMS_EPS = 1e-6
NEG = -1e30
HGRN_SUB = 16
SUBLANES = 8
V7X_VMEM_LIMIT = 56 * 1024 * 1024


def _bucket_thresholds():
    max_exact = REL_BUCKETS // 2
    d = np.arange(0, 4 * REL_MAX_DIST, dtype=np.int64)
    large = max_exact + (np.log(np.maximum(d, 1) / max_exact) / math.log(REL_MAX_DIST / max_exact)
                         * (REL_BUCKETS - max_exact)).astype(np.int64)
    bucket = np.where(d < max_exact, d, np.minimum(large, REL_BUCKETS - 1))
    assert np.all(np.diff(bucket) >= 0)
    return [int(np.argmax(bucket >= j)) for j in range(REL_BUCKETS)]


_BUCKET_THR = _bucket_thresholds()


def _bias_from_dist(dist, tab_ref, head):
    val = jnp.full(dist.shape, tab_ref[0, head], F32)
    for j in range(1, REL_BUCKETS):
        val = jnp.where(dist >= _BUCKET_THR[j], tab_ref[j, head], val)
    return val


def _dot_nt(a, b):
    return lax.dot_general(a, b, (((1,), (1,)), ((), ())), preferred_element_type=F32)


def _dot_tn(a, b):
    return lax.dot_general(a, b, (((0,), (0,)), ((), ())), preferred_element_type=F32)


def _dot(a, b):
    return jnp.dot(a, b, preferred_element_type=F32)


def _lane_head(shape):
    return lax.shift_right_logical(lax.broadcasted_iota(jnp.int32, shape, 1), 6)


def _stack_heads(q):
    lh = _lane_head(q.shape)
    zero = jnp.zeros_like(q)
    return jnp.concatenate([jnp.where(lh == h, q, zero) for h in range(N_HEADS_BR)], axis=0)


def _merge_heads(o4, n):
    lh = _lane_head((n, BRANCH_W))
    out = jnp.zeros((n, BRANCH_W), F32)
    for h in range(N_HEADS_BR):
        out = jnp.where(lh == h, o4[h * n:(h + 1) * n, :], out)
    return out


def _params(*sem):
    return pltpu.CompilerParams(dimension_semantics=sem, vmem_limit_bytes=V7X_VMEM_LIMIT)


_BF16_SLOTS = (0, 1, 2, 7, 8, 9)
_MOBA_V_SLOT = 2


def _inproj_kernel(x_ref, w_ref, wvt_ref, *out_refs):
    xb = x_ref[...].astype(BF16)
    blk = MOBA_BLOCK
    for s in range(N_IN_SLOTS):
        if s == _MOBA_V_SLOT:
            vt = _dot_nt(wvt_ref[...], xb).astype(BF16)
            for j in range(xb.shape[0] // blk):
                out_refs[s][j] = vt[:, j * blk:(j + 1) * blk]
        else:
            r = _dot(xb, w_ref[:, s * BRANCH_W:(s + 1) * BRANCH_W])
            out_refs[s][...] = r.astype(out_refs[s].dtype)
    g0 = N_IN_SLOTS * BRANCH_W
    out_refs[N_IN_SLOTS][...] = _dot(xb, w_ref[:, g0:])


def _inproj(xf, w, wvt):
    t, d = xf.shape
    tm = 512
    ncol = w.shape[1]
    blk = MOBA_BLOCK
    shapes = [jax.ShapeDtypeStruct((t, BRANCH_W), BF16 if s in _BF16_SLOTS else F32)
              for s in range(N_IN_SLOTS)]
    shapes.append(jax.ShapeDtypeStruct((t, ncol - N_IN_SLOTS * BRANCH_W), F32))
    out_specs = [pl.BlockSpec((tm, sh.shape[1]), lambda i: (i, 0)) for sh in shapes]
    shapes[_MOBA_V_SLOT] = jax.ShapeDtypeStruct((t // blk, BRANCH_W, blk), BF16)
    out_specs[_MOBA_V_SLOT] = pl.BlockSpec((tm // blk, BRANCH_W, blk), lambda i: (i, 0, 0))
    return pl.pallas_call(
        _inproj_kernel,
        grid=(t // tm,),
        in_specs=[pl.BlockSpec((tm, d), lambda i: (i, 0)),
                  pl.BlockSpec((d, ncol), lambda i: (0, 0)),
                  pl.BlockSpec((BRANCH_W, d), lambda i: (0, 0))],
        out_specs=out_specs,
        out_shape=shapes,
        compiler_params=_params("arbitrary"),
        name="inproj",
    )(xf, w, wvt)


_MOBA_BIAS_TILES = REL_MAX_DIST // MOBA_BLOCK + 2


def _moba_kernel(tab_ref, q_ref, k_ref, vt_ref, o_ref, bias_sc, kmean_sc, sel_sc, acc_sc, m_sc, l_sc):
    b = pl.program_id(0)
    i = pl.program_id(1)
    blk = MOBA_BLOCK
    nkb = k_ref.shape[0] // blk
    nh = N_HEADS_BR
    hd = HEAD_DIM

    @pl.when((b == 0) & (i == 0))
    def _():
        key = lax.broadcasted_iota(jnp.int32, (blk, blk), 0)
        qry = lax.broadcasted_iota(jnp.int32, (blk, blk), 1)
        for m in range(_MOBA_BIAS_TILES):
            for h in range(nh):
                if m == _MOBA_BIAS_TILES - 1:
                    tile = jnp.full((blk, blk), tab_ref[REL_BUCKETS - 1, h], F32)
                else:
                    dist = jnp.maximum(m * blk + qry - key, 0)
                    tile = _bias_from_dist(dist, tab_ref, h)
                if m == 0:
                    tile = jnp.where(qry >= key, tile, NEG)
                bias_sc[m, :, h * blk:(h + 1) * blk] = tile

    @pl.when(i == 0)
    def _():
        for n in range(nkb):
            kb = k_ref[n * blk:(n + 1) * blk, :].astype(F32)
            kmean_sc[n:n + 1, :] = jnp.sum(kb, axis=0, keepdims=True) * (1.0 / blk)

    q4 = _stack_heads(q_ref[...] * jnp.asarray(HEAD_DIM ** -0.5, BF16))

    km = kmean_sc[...]
    km_hi = km.astype(BF16)
    km_lo = (km - km_hi.astype(F32)).astype(BF16)
    gate = _dot_nt(km_hi, q4) + _dot_nt(km_lo, q4)
    blk_id = lax.broadcasted_iota(jnp.int32, gate.shape, 0)
    gate = jnp.where(blk_id < i, gate, -jnp.inf)
    sel = jnp.zeros(gate.shape, F32)
    for _ in range(MOBA_TOPK):
        mx = jnp.max(gate, axis=0, keepdims=True)
        idx = jnp.min(jnp.where(gate == mx, blk_id, nkb), axis=0, keepdims=True)
        pick = (blk_id == idx) & (mx > -jnp.inf)
        sel = jnp.where(pick, 1.0, sel)
        gate = jnp.where(pick, -jnp.inf, gate)
    sel_sc[...] = sel

    def weighted_values(p, n):
        pb = p.astype(BF16)
        vt = vt_ref[n]
        return jnp.concatenate(
            [_dot(vt[h * hd:(h + 1) * hd, :], pb[:, h * blk:(h + 1) * blk]) for h in range(nh)], axis=1)

    r0 = pl.multiple_of(i * blk, blk)
    s = _dot_nt(k_ref[pl.ds(r0, blk), :], q4) + bias_sc[0]
    m0 = jnp.max(s, axis=0, keepdims=True)
    p = jnp.exp(s - m0)
    m_sc[...] = m0
    l_sc[...] = jnp.sum(p, axis=0, keepdims=True)
    acc_sc[...] = weighted_values(p, i)

    def body(n, carry):
        k0 = pl.multiple_of(n * blk, blk)
        bias = bias_sc[jnp.minimum(i - n, _MOBA_BIAS_TILES - 1)]
        chosen = sel_sc[pl.ds(n, 1), :] > 0.0
        s = _dot_nt(k_ref[pl.ds(k0, blk), :], q4) + bias
        s = jnp.where(chosen, s, NEG)
        m_old = m_sc[...]
        m_new = jnp.maximum(m_old, jnp.max(s, axis=0, keepdims=True))
        alpha = jnp.exp(m_old - m_new)
        p = jnp.exp(s - m_new)
        l_sc[...] = alpha * l_sc[...] + jnp.sum(p, axis=0, keepdims=True)
        acc_sc[...] = alpha * acc_sc[...] + weighted_values(p, n)
        m_sc[...] = m_new
        return carry

    lax.fori_loop(0, i, body, 0)

    ot = acc_sc[...] / l_sc[...]
    drow = lax.broadcasted_iota(jnp.int32, (hd, BRANCH_W), 0)
    lane = lax.broadcasted_iota(jnp.int32, (hd, BRANCH_W), 1)
    out = jnp.zeros((blk, BRANCH_W), F32)
    for h in range(nh):
        place = jnp.where(lane == drow + h * hd, 1.0, 0.0).astype(BF16)
        piece = ot[:, h * blk:(h + 1) * blk]
        hi = piece.astype(BF16)
        lo = (piece - hi.astype(F32)).astype(BF16)
        out = out + _dot_tn(hi, place) + _dot_tn(lo, place)
    o_ref[...] = out


def _moba(tab, q, k, vt):
    bsz, s, w = q.shape
    blk = MOBA_BLOCK
    nkb = s // blk
    nh = N_HEADS_BR
    return pl.pallas_call(
        _moba_kernel,
        grid=(bsz, nkb),
        in_specs=[pl.BlockSpec(memory_space=pltpu.SMEM),
                  pl.BlockSpec((None, blk, w), lambda b, i: (b, i, 0)),
                  pl.BlockSpec((None, s, w), lambda b, i: (b, 0, 0)),
                  pl.BlockSpec((None, nkb, w, blk), lambda b, i: (b, 0, 0, 0))],
        out_specs=pl.BlockSpec((None, blk, w), lambda b, i: (b, i, 0)),
        out_shape=jax.ShapeDtypeStruct((bsz, s, w), F32),
        scratch_shapes=[pltpu.VMEM((_MOBA_BIAS_TILES, blk, nh * blk), F32),
                        pltpu.VMEM((nkb, w), F32),
                        pltpu.VMEM((nkb, nh * blk), F32),
                        pltpu.VMEM((HEAD_DIM, nh * blk), F32),
                        pltpu.VMEM((1, nh * blk), F32),
                        pltpu.VMEM((1, nh * blk), F32)],
        compiler_params=_params("arbitrary", "arbitrary"),
        name="moba",
    )(tab, q, k, vt)


def _dil_kernel(tab_ref, q_ref, kc_ref, kp_ref, vc_ref, vp_ref, o_ref, lse_ref, bias_sc, *, dil, tq):
    nb = DIL_BLOCK
    nh = N_HEADS_BR
    t = pl.program_id(2)
    first = (pl.program_id(0) == 0) & (pl.program_id(1) == 0) & (t == 0)

    @pl.when(first)
    def _():
        row = lax.broadcasted_iota(jnp.int32, (nb, 2 * nb), 0)
        col = lax.broadcasted_iota(jnp.int32, (nb, 2 * nb), 1)
        dist_sub = nb + row - col
        band = (dist_sub >= 0) & (dist_sub <= nb)
        for h in range(nh):
            tile = _bias_from_dist(jnp.maximum(dist_sub, 0) * dil, tab_ref, nh + h)
            bias_sc[h * nb:(h + 1) * nb, :] = jnp.where(band, tile, NEG)

    scale = jnp.asarray(HEAD_DIM ** -0.5, BF16)
    for jb in range(tq // nb):
        lo, hi = jb * nb, (jb + 1) * nb
        q4 = _stack_heads(q_ref[lo:hi, :] * scale)
        if jb == 0:
            kprev, vprev = kp_ref[tq - nb:tq, :], vp_ref[tq - nb:tq, :]
        else:
            kprev, vprev = kc_ref[lo - nb:lo, :], vc_ref[lo - nb:lo, :]
        keys = jnp.concatenate([kprev, kc_ref[lo:hi, :]], axis=0)
        vals = jnp.concatenate([vprev, vc_ref[lo:hi, :]], axis=0)
        s = _dot_nt(q4, keys) + bias_sc[...]
        if jb == 0:
            col = lax.broadcasted_iota(jnp.int32, s.shape, 1)
            s = jnp.where((col >= nb) | (t > 0), s, NEG)
        m = jnp.max(s, axis=-1, keepdims=True)
        p = jnp.exp(s - m)
        l = jnp.sum(p, axis=-1, keepdims=True)
        o4 = _dot(p.astype(BF16), vals) / l
        lse4 = jnp.broadcast_to(m + jnp.log(l), o4.shape)
        o_ref[lo:hi, :] = _merge_heads(o4, nb)
        lse_ref[lo:hi, :] = _merge_heads(lse4, nb)


def _dilated(tab, q, k, v, dil):
    bsz, s, w = q.shape
    sub_len = s // dil
    tq = min(512, sub_len)
    nt = sub_len // tq
    view = lambda a: a.reshape(bsz, sub_len, dil * w)
    cur = pl.BlockSpec((None, tq, w), lambda b, r, t: (b, t, r))
    prev = pl.BlockSpec((None, tq, w), lambda b, r, t: (b, jnp.maximum(t - 1, 0), r))
    out_shape = jax.ShapeDtypeStruct((bsz, sub_len, dil * w), F32)
    o, lse = pl.pallas_call(
        functools.partial(_dil_kernel, dil=dil, tq=tq),
        grid=(bsz, dil, nt),
        in_specs=[pl.BlockSpec(memory_space=pltpu.SMEM), cur, cur, prev, cur, prev],
        out_specs=[cur, cur],
        out_shape=[out_shape, out_shape],
        scratch_shapes=[pltpu.VMEM((N_HEADS_BR * DIL_BLOCK, 2 * DIL_BLOCK), F32)],
        compiler_params=_params("arbitrary", "arbitrary", "arbitrary"),
        name=f"dilated{dil}",
    )(tab, view(q), view(k), view(k), view(v), view(v))
    return o.reshape(bsz, s, w), lse.reshape(bsz, s, w)


_S5_STEPS = (1, 2, 4)


def _s5_kernel(u_ref, wb_ref, coef_ref, wc_ref, d_ref, gw_ref, gb_ref, o_ref, h_sc, carry_sc):
    ts = u_ref.shape[0]
    nl = S5_LANES

    @pl.when(pl.program_id(1) == 0)
    def _():
        carry_sc[...] = jnp.zeros_like(carry_sc)

    u = u_ref[...]
    h_sc[...] = _dot(u.astype(BF16), wb_ref[...])

    def cfma(re, im, cr, ci, sre, sim):
        return re + cr * sre - ci * sim, im + cr * sim + ci * sre

    def body(g, carry):
        cre, cim = carry
        r0 = pl.multiple_of(g * SUBLANES, SUBLANES)
        re = h_sc[pl.ds(r0, SUBLANES), :nl]
        im = h_sc[pl.ds(r0, SUBLANES), nl:]
        for si, k in enumerate(_S5_STEPS):
            re, im = cfma(re, im, coef_ref[si, 0], coef_ref[si, 1],
                          pltpu.roll(re, k, axis=0), pltpu.roll(im, k, axis=0))
        last = len(_S5_STEPS)
        re, im = cfma(re, im, coef_ref[last, 0], coef_ref[last, 1],
                      jnp.broadcast_to(cre, re.shape), jnp.broadcast_to(cim, im.shape))
        h_sc[pl.ds(r0, SUBLANES), :nl] = re
        h_sc[pl.ds(r0, SUBLANES), nl:] = im
        return re[SUBLANES - 1:, :], im[SUBLANES - 1:, :]

    cre, cim = lax.fori_loop(0, ts // SUBLANES, body, (carry_sc[0:1, :nl], carry_sc[0:1, nl:]))
    carry_sc[0:1, :nl] = cre
    carry_sc[0:1, nl:] = cim

    y = _dot(h_sc[...].astype(BF16), wc_ref[...]) + d_ref[...] * u
    y = 0.5 * y * (1.0 + jnp.tanh(math.sqrt(2.0 / math.pi) * (y + 0.044715 * (y * y * y))))
    z = _dot(y.astype(BF16), gw_ref[...]) + gb_ref[...]
    o_ref[...] = y * jax.nn.sigmoid(z)


def _s5_weights(a_re, a_im, log_dt, b_re, b_im, c_re, c_im):
    g, p, c = b_re.shape
    ar, ai = a_re.astype(F32), a_im.astype(F32)
    dt = jnp.exp(log_dt.astype(F32))[:, None]

    def abar_pow(k):
        mag = jnp.exp(k * dt * ar)
        return (mag * jnp.cos(k * dt * ai)).reshape(-1), (mag * jnp.sin(k * dt * ai)).reshape(-1)

    abar_re, abar_im = jnp.exp(dt * ar) * jnp.cos(dt * ai), jnp.exp(dt * ar) * jnp.sin(dt * ai)
    nr, ni = abar_re - 1.0, abar_im
    den = ar * ar + ai * ai
    zr = (nr * ar + ni * ai) / den
    zi = (ni * ar - nr * ai) / den
    br, bi = b_re.astype(F32), b_im.astype(F32)
    bbar_re = zr[..., None] * br - zi[..., None] * bi
    bbar_im = zr[..., None] * bi + zi[..., None] * br
    eye = jnp.eye(g, dtype=F32)
    blockdiag_in = lambda m: jnp.einsum('gpc,gh->gchp', m, eye).reshape(g * c, g * p)
    wb = jnp.concatenate([blockdiag_in(bbar_re), blockdiag_in(bbar_im)], axis=1)
    blockdiag_out = lambda m: jnp.einsum('gcp,gh->gphc', m, eye).reshape(g * p, g * c)
    wc = jnp.concatenate([blockdiag_out(c_re.astype(F32)), -blockdiag_out(c_im.astype(F32))], axis=0)
    rows = jnp.arange(SUBLANES)[:, None]
    coefs = []
    for k in _S5_STEPS:
        pr, pi = abar_pow(k)
        coefs.append(jnp.stack([jnp.where(rows >= k, pr[None, :], 0.0),
                                jnp.where(rows >= k, pi[None, :], 0.0)]))
    per_row = [abar_pow(k + 1) for k in range(SUBLANES)]
    coefs.append(jnp.stack([jnp.stack([r for r, _ in per_row]), jnp.stack([i for _, i in per_row])]))
    return wb.astype(BF16), jnp.stack(coefs), wc.astype(BF16)


def _s5(u, wb, coef, wc, d_skip, glu_w, glu_b):
    bsz, s, w = u.shape
    ts = min(512, s)
    const = lambda shape: pl.BlockSpec(shape, lambda b, t: (0,) * len(shape))
    return pl.pallas_call(
        _s5_kernel,
        grid=(bsz, s // ts),
        in_specs=[pl.BlockSpec((None, ts, w), lambda b, t: (b, t, 0)),
                  const(wb.shape), const(coef.shape), const(wc.shape),
                  const((1, w)), const(glu_w.shape), const((1, w))],
        out_specs=pl.BlockSpec((None, ts, w), lambda b, t: (b, t, 0)),
        out_shape=jax.ShapeDtypeStruct((bsz, s, w), F32),
        scratch_shapes=[pltpu.VMEM((ts, 2 * S5_LANES), F32),
                        pltpu.VMEM((SUBLANES, 2 * S5_LANES), F32)],
        compiler_params=_params("arbitrary", "arbitrary"),
        name="s5",
    )(u, wb, coef, wc, d_skip.reshape(1, w), glu_w, glu_b.reshape(1, w))


def _hgrn_kernel(lb_ref, q_ref, f_ref, i_ref, o_ref, st_sc):
    th = q_ref.shape[0]
    c = HGRN_SUB
    w = BRANCH_W

    @pl.when(pl.program_id(1) == 0)
    def _():
        st_sc[...] = jnp.zeros_like(st_sc)

    lb = lb_ref[...]
    rowi = lax.broadcasted_iota(jnp.int32, (c, w), 0)
    same_head = (lax.shift_right_logical(lax.broadcasted_iota(jnp.int32, (w, w), 0), 6)
                 == lax.shift_right_logical(lax.broadcasted_iota(jnp.int32, (w, w), 1), 6))
    head_ones = jnp.where(same_head, 1.0, 0.0).astype(BF16)

    def body(ci, carry):
        r0 = pl.multiple_of(ci * c, c)
        q = q_ref[pl.ds(r0, c), :]
        qf = q * jax.nn.sigmoid(q) * (HEAD_DIM ** -0.5)
        f = lb + (1.0 - lb) * jax.nn.sigmoid(f_ref[pl.ds(r0, c), :])
        kk = 1.0 - f
        v = i_ref[pl.ds(r0, c), :]
        b = jnp.log(f)
        k = 1
        while k < c:
            b = b + jnp.where(rowi >= k, pltpu.roll(b, k, axis=0), 0.0)
            k *= 2
        e = jnp.concatenate(
            [jnp.exp(jnp.where(rowi >= s, b - b[s:s + 1, :], NEG)) * (qf * kk[s:s + 1, :])
             for s in range(c)], axis=0)
        wgt = _dot(e.astype(BF16), head_ones)
        o = jnp.zeros((c, w), F32)
        for s in range(c):
            o = o + wgt[s * c:(s + 1) * c, :] * v[s:s + 1, :]
        st = st_sc[...]
        o = o + _dot_nt((qf * jnp.exp(b)).astype(BF16), st.astype(BF16))
        b_last = b[c - 1:c, :]
        upd = _dot_tn(v.astype(BF16), (kk * jnp.exp(b_last - b)).astype(BF16))
        st_sc[...] = jnp.exp(b_last) * st + jnp.where(same_head, upd, 0.0)
        o_ref[pl.ds(r0, c), :] = o
        return carry

    lax.fori_loop(0, th // c, body, 0)


def _hgrn(lb, q, f, i):
    bsz, s, w = q.shape
    th = min(512, s)
    tile = pl.BlockSpec((None, th, w), lambda b, t: (b, t, 0))
    return pl.pallas_call(
        _hgrn_kernel,
        grid=(bsz, s // th),
        in_specs=[pl.BlockSpec((1, w), lambda b, t: (0, 0)), tile, tile, tile],
        out_specs=tile,
        out_shape=jax.ShapeDtypeStruct((bsz, s, w), F32),
        scratch_shapes=[pltpu.VMEM((w, w), F32)],
        compiler_params=_params("arbitrary", "arbitrary"),
        name="hgrn2",
    )(lb.reshape(1, w), q, f, i)


def _outproj_kernel(ya_ref, yb_ref, yc_ref, o1_ref, o2_ref, o3_ref, l1_ref, l2_ref, l3_ref,
                    gates_ref, x_ref, w_ref, gain_ref, lng_ref, lnb_ref, out_ref, *, alpha):
    l1, l2, l3 = l1_ref[...], l2_ref[...], l3_ref[...]
    lm = jnp.maximum(jnp.maximum(l1, l2), l3)
    e1, e2, e3 = jnp.exp(l1 - lm), jnp.exp(l2 - lm), jnp.exp(l3 - lm)
    yd = (e1 * o1_ref[...] + e2 * o2_ref[...] + e3 * o3_ref[...]) / (e1 + e2 + e3)

    w = BRANCH_W
    same_head = (lax.shift_right_logical(lax.broadcasted_iota(jnp.int32, (w, w), 0), 6)
                 == lax.shift_right_logical(lax.broadcasted_iota(jnp.int32, (w, w), 1), 6))
    head_ones = jnp.where(same_head, 1.0, 0.0).astype(BF16)
    normed = []
    for y in (ya_ref[...], yb_ref[...], yc_ref[...], yd):
        y2 = y * y
        hi = y2.astype(BF16)
        lo = (y2 - hi.astype(F32)).astype(BF16)
        ss = _dot(hi, head_ones) + _dot(lo, head_ones)
        normed.append(y * lax.rsqrt(ss * (1.0 / HEAD_DIM) + RMS_EPS))
    g = gates_ref[...]
    y = jnp.concatenate(normed, axis=-1) * gain_ref[...] * (g * jax.nn.sigmoid(g))
    r = alpha * x_ref[...] + _dot(y.astype(BF16), w_ref[...])
    mu = jnp.mean(r, axis=-1, keepdims=True)
    rc = r - mu
    var = jnp.mean(rc * rc, axis=-1, keepdims=True)
    out_ref[...] = rc * lax.rsqrt(var + LN_EPS) * lng_ref[...] + lnb_ref[...]


def _outproj(branches, gates, xf, w, gain, ln_g, ln_b, alpha):
    t, d = xf.shape
    tm = 256
    wide = pl.BlockSpec((tm, d), lambda i: (i, 0))
    narrow = pl.BlockSpec((tm, BRANCH_W), lambda i: (i, 0))
    vec = pl.BlockSpec((1, d), lambda i: (0, 0))
    return pl.pallas_call(
        functools.partial(_outproj_kernel, alpha=alpha),
        grid=(t // tm,),
        in_specs=[narrow] * 9 + [wide, wide, pl.BlockSpec((d, d), lambda i: (0, 0)), vec, vec, vec],
        out_specs=wide,
        out_shape=jax.ShapeDtypeStruct((t, d), F32),
        compiler_params=_params("arbitrary"),
        name="outproj",
    )(*branches, gates, xf, w, gain.reshape(1, d), ln_g.reshape(1, d), ln_b.reshape(1, d))


def kernel(x, w_in, rel_bias, s5_a_re, s5_a_im, s5_log_dt, s5_b_re, s5_b_im, s5_c_re, s5_c_im,
           s5_d, s5_glu_w, s5_glu_b, hgrn_lower, branch_gain, w_out, ln_g, ln_b):
    bsz, s, d = x.shape
    depth = w_in.shape[0]
    t = bsz * s
    w = BRANCH_W
    alpha = (2 * depth) ** 0.25
    p_lb = jax.nn.softmax(hgrn_lower.astype(F32), axis=0)
    lb_all = jnp.cumsum(p_lb, axis=0) - p_lb[0]
    tab = rel_bias.astype(F32)
    w_in_b = w_in.astype(BF16)
    w_out_b = w_out.astype(BF16)
    glu_w_b = s5_glu_w.astype(BF16)

    xf = x.reshape(t, d).astype(F32)
    for l in range(depth):
        wvt = w_in_b[l][:, _MOBA_V_SLOT * w:(_MOBA_V_SLOT + 1) * w].T
        qa, ka, vat, us, qc, fc, ic, qd, kd, vd, gates = _inproj(xf, w_in_b[l], wvt)
        seq = lambda a: a.reshape(bsz, s, w)
        ya = _moba(tab, seq(qa), seq(ka), vat.reshape(bsz, s // MOBA_BLOCK, w, MOBA_BLOCK))
        wb, coef, wc = _s5_weights(s5_a_re[l], s5_a_im[l], s5_log_dt[l], s5_b_re[l], s5_b_im[l],
                                   s5_c_re[l], s5_c_im[l])
        yb = _s5(seq(us), wb, coef, wc, s5_d[l].astype(F32), glu_w_b[l], s5_glu_b[l].astype(F32))
        yc = _hgrn(lb_all[l], seq(qc), seq(fc), seq(ic))
        dil_out = [_dilated(tab, seq(qd), seq(kd), seq(vd), dil) for _, dil in DIL_PAIRS]
        flat = lambda a: a.reshape(t, w)
        branches = ([flat(ya), flat(yb), flat(yc)] + [flat(o) for o, _ in dil_out]
                    + [flat(lse) for _, lse in dil_out])
        xf = _outproj(branches, gates, xf, w_out_b[l], branch_gain[l].astype(F32),
                      ln_g[l].astype(F32), ln_b[l].astype(F32), alpha)
    return xf.reshape(bsz, s, d).astype(x.dtype)
```

```python
import functools
import math

import numpy as np
import jax
import jax.numpy as jnp
from jax import lax
from jax.experimental import pallas as pl
from jax.experimental.pallas import tpu as pltpu

F32 = jnp.float32
BF16 = jnp.bfloat16

HEAD_DIM = 64
N_HEADS_BR = 4
BRANCH_W = N_HEADS_BR * HEAD_DIM
N_IN_SLOTS = 10
MOBA_BLOCK = 256
MOBA_TOPK = 3
S5_GROUP = 16
S5_GROUPS = BRANCH_W // S5_GROUP
S5_STATE = 64
S5_LANES = S5_GROUPS * S5_STATE
DIL_PAIRS = ((128, 1), (512, 4), (2048, 16))
DIL_BLOCK = 128
REL_BUCKETS = 32
REL_MAX_DIST = 2048
LN_EPS = 1e-5
RMS_EPS = 1e-6
NEG = -1e30
HGRN_SUB = 16
SUBLANES = 8
LANES = 128
BF16_ROWS = 16
V7X_VMEM_LIMIT = 56 * 1024 * 1024
LOG2E = math.log2(math.e)
LN2 = math.log(2.0)
QK_SCALE = HEAD_DIM ** -0.5 * LOG2E
IN_TILE = 512
OUT_TILE = 256


def _bucket_thresholds():
    max_exact = REL_BUCKETS // 2
    d = np.arange(0, 4 * REL_MAX_DIST, dtype=np.int64)
    large = max_exact + (np.log(np.maximum(d, 1) / max_exact) / math.log(REL_MAX_DIST / max_exact)
                         * (REL_BUCKETS - max_exact)).astype(np.int64)
    bucket = np.where(d < max_exact, d, np.minimum(large, REL_BUCKETS - 1))
    assert np.all(np.diff(bucket) >= 0)
    return [int(np.argmax(bucket >= j)) for j in range(REL_BUCKETS)]


_BUCKET_THR = _bucket_thresholds()


def _bias_from_dist(dist, tab_ref, head):
    val = jnp.full(dist.shape, tab_ref[0, head] * LOG2E, F32)
    for j in range(1, REL_BUCKETS):
        val = jnp.where(dist >= _BUCKET_THR[j], tab_ref[j, head] * LOG2E, val)
    return val


def _dot_nt(a, b):
    return lax.dot_general(a, b, (((1,), (1,)), ((), ())), preferred_element_type=F32)


def _dot_tn(a, b):
    return lax.dot_general(a, b, (((0,), (0,)), ((), ())), preferred_element_type=F32)


def _dot(a, b):
    return jnp.dot(a, b, preferred_element_type=F32)


def _lane_head(shape):
    return lax.shift_right_logical(lax.broadcasted_iota(jnp.int32, shape, 1), 6)


def _stack_heads(q):
    lh = _lane_head(q.shape)
    zero = jnp.zeros_like(q)
    return jnp.concatenate([jnp.where(lh == h, q, zero) for h in range(N_HEADS_BR)], axis=0)


def _merge_heads(o4, n):
    lh = _lane_head((n, BRANCH_W))
    out = jnp.zeros((n, BRANCH_W), F32)
    for h in range(N_HEADS_BR):
        out = jnp.where(lh == h, o4[h * n:(h + 1) * n, :], out)
    return out


def _same_head_ones():
    w = BRANCH_W
    same = (lax.shift_right_logical(lax.broadcasted_iota(jnp.int32, (w, w), 0), 6)
            == lax.shift_right_logical(lax.broadcasted_iota(jnp.int32, (w, w), 1), 6))
    return same, jnp.where(same, 1.0, 0.0).astype(BF16)


def _params(*sem):
    return pltpu.CompilerParams(dimension_semantics=sem, vmem_limit_bytes=V7X_VMEM_LIMIT)


_BF16_SLOTS = (0, 1, 2, 7, 8, 9)
_Q_SLOTS = (0, 7)
_MOBA_V_SLOT = 2
_DIL_SLOTS = (7, 8, 9)
_RESIDUE_DILS = tuple(dil for _, dil in DIL_PAIRS if dil > 1)


def _inproj_kernel(x_ref, w_ref, wvt_ref, *refs):
    n_plain = N_IN_SLOTS + 1
    out_refs = refs[:n_plain]
    res_refs = refs[n_plain:n_plain + len(_DIL_SLOTS) * len(_RESIDUE_DILS)]
    slab_sc = refs[-1]
    xb = x_ref[...].astype(BF16)
    tm = xb.shape[0]
    blk = MOBA_BLOCK
    for s in range(N_IN_SLOTS):
        if s == _MOBA_V_SLOT:
            vt = _dot_nt(wvt_ref[...], xb).astype(BF16)
            for j in range(tm // blk):
                out_refs[s][j] = vt[:, j * blk:(j + 1) * blk]
            continue
        r = _dot(xb, w_ref[:, s * BRANCH_W:(s + 1) * BRANCH_W])
        if s in _Q_SLOTS:
            r = r * QK_SCALE
        out_refs[s][...] = r.astype(out_refs[s].dtype)
        if s in _DIL_SLOTS:
            for half in range(BRANCH_W // LANES):
                slab_sc[half] = r[:, half * LANES:(half + 1) * LANES]
            for di, dil in enumerate(_RESIDUE_DILS):
                o_ref = res_refs[_DIL_SLOTS.index(s) * len(_RESIDUE_DILS) + di]
                for rr in range(dil):
                    for half in range(BRANCH_W // LANES):
                        c0 = rr * BRANCH_W + half * LANES
                        o_ref[:, c0:c0 + LANES] = slab_sc[half, pl.ds(rr, tm // dil, stride=dil), :].astype(BF16)
    g0 = N_IN_SLOTS * BRANCH_W
    out_refs[N_IN_SLOTS][...] = _dot(xb, w_ref[:, g0:])


def _inproj(xf, w, wvt):
    t, d = xf.shape
    tm = IN_TILE
    ncol = w.shape[1]
    blk = MOBA_BLOCK
    shapes = [jax.ShapeDtypeStruct((t, BRANCH_W), BF16 if s in _BF16_SLOTS else F32)
              for s in range(N_IN_SLOTS)]
    shapes.append(jax.ShapeDtypeStruct((t, ncol - N_IN_SLOTS * BRANCH_W), F32))
    out_specs = [pl.BlockSpec((tm, sh.shape[1]), lambda i: (i, 0)) for sh in shapes]
    shapes[_MOBA_V_SLOT] = jax.ShapeDtypeStruct((t // blk, BRANCH_W, blk), BF16)
    out_specs[_MOBA_V_SLOT] = pl.BlockSpec((tm // blk, BRANCH_W, blk), lambda i: (i, 0, 0))
    for _ in _DIL_SLOTS:
        for dil in _RESIDUE_DILS:
            shapes.append(jax.ShapeDtypeStruct((t // dil, dil * BRANCH_W), BF16))
            out_specs.append(pl.BlockSpec((tm // dil, dil * BRANCH_W), lambda i: (i, 0)))
    outs = pl.pallas_call(
        _inproj_kernel,
        grid=(t // tm,),
        in_specs=[pl.BlockSpec((tm, d), lambda i: (i, 0)),
                  pl.BlockSpec((d, ncol), lambda i: (0, 0)),
                  pl.BlockSpec((BRANCH_W, d), lambda i: (0, 0))],
        out_specs=out_specs,
        out_shape=shapes,
        scratch_shapes=[pltpu.VMEM((BRANCH_W // LANES, tm, LANES), F32)],
        compiler_params=_params("arbitrary"),
        name="inproj",
    )(xf, w, wvt)
    plain, res = outs[:N_IN_SLOTS + 1], outs[N_IN_SLOTS + 1:]
    nres = len(_RESIDUE_DILS)
    by_dil = {dil: [res[si * nres + di] for si in range(len(_DIL_SLOTS))]
              for di, dil in enumerate(_RESIDUE_DILS)}
    return plain, by_dil


_MOBA_BIAS_TILES = REL_MAX_DIST // MOBA_BLOCK + 2
_MASKED_OUT = 1e30


def _moba_kernel(tab_ref, q_ref, k_ref, vt_ref, o_ref,
                 bias_sc, kmean_sc, q4_sc, sel_sc, s0_sc, s1_sc, acc_sc, m_sc, l_sc):
    b = pl.program_id(0)
    i = pl.program_id(1)
    blk = MOBA_BLOCK
    nkb = k_ref.shape[0] // blk
    nh = N_HEADS_BR
    hd = HEAD_DIM

    @pl.when((b == 0) & (i == 0))
    def _():
        key = lax.broadcasted_iota(jnp.int32, (blk, blk), 0)
        qry = lax.broadcasted_iota(jnp.int32, (blk, blk), 1)
        for m in range(_MOBA_BIAS_TILES):
            for h in range(nh):
                if m == _MOBA_BIAS_TILES - 1:
                    tile = jnp.full((blk, blk), tab_ref[REL_BUCKETS - 1, h] * LOG2E, F32)
                else:
                    tile = _bias_from_dist(jnp.maximum(m * blk + qry - key, 0), tab_ref, h)
                if m == 0:
                    tile = jnp.where(qry >= key, tile, NEG)
                bias_sc[m, :, h * blk:(h + 1) * blk] = tile

    @pl.when(i == 0)
    def _():
        for n in range(nkb):
            kb = k_ref[n * blk:(n + 1) * blk, :].astype(F32)
            kmean_sc[n:n + 1, :] = jnp.sum(kb, axis=0, keepdims=True) * (1.0 / blk)

    q4_sc[...] = _stack_heads(q_ref[...])
    q4 = q4_sc[...]

    km = kmean_sc[...]
    km_hi = km.astype(BF16)
    km_lo = (km - km_hi.astype(F32)).astype(BF16)
    gate = _dot_nt(km_hi, q4) + _dot_nt(km_lo, q4)
    blk_id = lax.broadcasted_iota(jnp.int32, gate.shape, 0)
    gate = jnp.where(blk_id < i, gate, -jnp.inf)
    sel = jnp.where(blk_id == i, 1.0, 0.0)
    for _ in range(MOBA_TOPK):
        mx = jnp.max(gate, axis=0, keepdims=True)
        idx = jnp.min(jnp.where(gate == mx, blk_id, nkb), axis=0, keepdims=True)
        pick = (blk_id == idx) & (mx > -jnp.inf)
        sel = jnp.where(pick, 1.0, sel)
        gate = jnp.where(pick, -jnp.inf, gate)
    sel_sc[...] = sel

    m_sc[...] = jnp.full(m_sc.shape, NEG, F32)
    l_sc[...] = jnp.zeros(l_sc.shape, F32)
    acc_sc[...] = jnp.zeros(acc_sc.shape, F32)
    ones_rows = jnp.ones((BF16_ROWS, blk), BF16)

    def scores(n, dst_sc):
        k0 = pl.multiple_of(jnp.minimum(n, nkb - 1) * blk, blk)
        dst_sc[...] = (_dot_nt(k_ref[pl.ds(k0, blk), :], q4_sc[...])
                       + bias_sc[jnp.clip(i - n, 0, _MOBA_BIAS_TILES - 1)])

    def visit(n, src_sc):
        nc = jnp.minimum(n, nkb - 1)
        s = src_sc[...]
        chosen = sel_sc[pl.ds(nc, 1), :] > 0.0
        m_old = m_sc[...]
        m_new = jnp.maximum(m_old, jnp.where(chosen, jnp.max(s, axis=0, keepdims=True), NEG))
        alpha = jnp.exp2(m_old - m_new)
        pb = jnp.exp2(s - jnp.where(chosen, m_new, _MASKED_OUT)).astype(BF16)
        vt = vt_ref[nc]
        pv = jnp.concatenate(
            [_dot(jnp.concatenate([vt[h * hd:(h + 1) * hd, :], ones_rows], axis=0),
                  pb[:, h * blk:(h + 1) * blk]) for h in range(nh)], axis=1)
        acc_sc[...] = alpha * acc_sc[...] + pv[:hd]
        l_sc[...] = alpha * l_sc[...] + pv[hd:hd + 1]
        m_sc[...] = m_new

    scores(0, s0_sc)

    def pair_body(j, carry):
        n = 2 * j
        scores(n + 1, s1_sc)
        visit(n, s0_sc)
        scores(n + 2, s0_sc)
        visit(n + 1, s1_sc)
        return carry

    lax.fori_loop(0, (i + 2) // 2, pair_body, 0)

    ot = acc_sc[...] / l_sc[...]
    drow = lax.broadcasted_iota(jnp.int32, (hd, BRANCH_W), 0)
    lane = lax.broadcasted_iota(jnp.int32, (hd, BRANCH_W), 1)
    out = jnp.zeros((blk, BRANCH_W), F32)
    for h in range(nh):
        place = jnp.where(lane == drow + h * hd, 1.0, 0.0).astype(BF16)
        piece = ot[:, h * blk:(h + 1) * blk]
        hi = piece.astype(BF16)
        lo = (piece - hi.astype(F32)).astype(BF16)
        out = out + _dot_tn(hi, place) + _dot_tn(lo, place)
    o_ref[...] = out


def _moba(tab, q, k, vt):
    bsz, s, w = q.shape
    blk = MOBA_BLOCK
    nkb = s // blk
    nh = N_HEADS_BR
    return pl.pallas_call(
        _moba_kernel,
        grid=(bsz, nkb),
        in_specs=[pl.BlockSpec(memory_space=pltpu.SMEM),
                  pl.BlockSpec((None, blk, w), lambda b, i: (b, i, 0)),
                  pl.BlockSpec((None, s, w), lambda b, i: (b, 0, 0)),
                  pl.BlockSpec((None, nkb, w, blk), lambda b, i: (b, 0, 0, 0))],
        out_specs=pl.BlockSpec((None, blk, w), lambda b, i: (b, i, 0)),
        out_shape=jax.ShapeDtypeStruct((bsz, s, w), F32),
        scratch_shapes=[pltpu.VMEM((_MOBA_BIAS_TILES, blk, nh * blk), F32),
                        pltpu.VMEM((nkb, w), F32),
                        pltpu.VMEM((nh * blk, w), BF16),
                        pltpu.VMEM((nkb, nh * blk), F32),
                        pltpu.VMEM((blk, nh * blk), F32),
                        pltpu.VMEM((blk, nh * blk), F32),
                        pltpu.VMEM((HEAD_DIM, nh * blk), F32),
                        pltpu.VMEM((1, nh * blk), F32),
                        pltpu.VMEM((1, nh * blk), F32)],
        compiler_params=_params("arbitrary", "arbitrary"),
        name="moba",
    )(tab, q, k, vt)


def _dil_kernel(tab_ref, q_ref, kc_ref, kp_ref, vc_ref, vp_ref, o_ref, lse_ref, bias_sc, *, dil, tq):
    nb = DIL_BLOCK
    nh = N_HEADS_BR
    t = pl.program_id(2)
    first = (pl.program_id(0) == 0) & (pl.program_id(1) == 0) & (t == 0)

    @pl.when(first)
    def _():
        row = lax.broadcasted_iota(jnp.int32, (nb, 2 * nb), 0)
        col = lax.broadcasted_iota(jnp.int32, (nb, 2 * nb), 1)
        dist_sub = nb + row - col
        band = (dist_sub >= 0) & (dist_sub <= nb)
        for h in range(nh):
            tile = _bias_from_dist(jnp.maximum(dist_sub, 0) * dil, tab_ref, nh + h)
            bias_sc[h * nb:(h + 1) * nb, :] = jnp.where(band, tile, NEG)

    for jb in range(tq // nb):
        lo, hi = jb * nb, (jb + 1) * nb
        q4 = _stack_heads(q_ref[lo:hi, :])
        if jb == 0:
            kprev, vprev = kp_ref[tq - nb:tq, :], vp_ref[tq - nb:tq, :]
        else:
            kprev, vprev = kc_ref[lo - nb:lo, :], vc_ref[lo - nb:lo, :]
        keys = jnp.concatenate([kprev, kc_ref[lo:hi, :]], axis=0)
        vals = jnp.concatenate([vprev, vc_ref[lo:hi, :]], axis=0)
        s = _dot_nt(q4, keys) + bias_sc[...]
        if jb == 0:
            col = lax.broadcasted_iota(jnp.int32, s.shape, 1)
            s = jnp.where((col >= nb) | (t > 0), s, NEG)
        m = jnp.max(s, axis=-1, keepdims=True)
        p = jnp.exp2(s - m)
        l = jnp.sum(p, axis=-1, keepdims=True)
        o4 = _dot(p.astype(BF16), vals) / l
        lse4 = jnp.broadcast_to((m + jnp.log2(l)) * LN2, o4.shape)
        o_ref[lo:hi, :] = _merge_heads(o4, nb)
        lse_ref[lo:hi, :] = _merge_heads(lse4, nb)


def _dilated(tab, q, k, v, dil, bsz):
    rows, width = q.shape
    w = BRANCH_W
    sub_len = rows // bsz
    tq = min(512, sub_len)
    nt = sub_len // tq
    view = lambda a: a.reshape(bsz, sub_len, width)
    cur = pl.BlockSpec((None, tq, w), lambda b, r, t: (b, t, r))
    prev = pl.BlockSpec((None, tq, w), lambda b, r, t: (b, jnp.maximum(t - 1, 0), r))
    out_shape = jax.ShapeDtypeStruct((bsz, sub_len, width), F32)
    o, lse = pl.pallas_call(
        functools.partial(_dil_kernel, dil=dil, tq=tq),
        grid=(bsz, dil, nt),
        in_specs=[pl.BlockSpec(memory_space=pltpu.SMEM), cur, cur, prev, cur, prev],
        out_specs=[cur, cur],
        out_shape=[out_shape, out_shape],
        scratch_shapes=[pltpu.VMEM((N_HEADS_BR * DIL_BLOCK, 2 * DIL_BLOCK), F32)],
        compiler_params=_params("arbitrary", "arbitrary", "arbitrary"),
        name=f"dilated{dil}",
    )(tab, view(q), view(k), view(k), view(v), view(v))
    return o.reshape(rows, width), lse.reshape(rows, width)


_S5_STEPS = (1, 2, 4)


def _s5_kernel(u_ref, wb_ref, coef_ref, wc_ref, d_ref, gw_ref, gb_ref, o_ref, h_sc, carry_sc):
    ts = u_ref.shape[0]
    nl = S5_LANES

    @pl.when(pl.program_id(1) == 0)
    def _():
        carry_sc[...] = jnp.zeros_like(carry_sc)

    u = u_ref[...]
    h_sc[...] = _dot(u.astype(BF16), wb_ref[...])

    def cfma(re, im, cr, ci, sre, sim):
        return re + cr * sre - ci * sim, im + cr * sim + ci * sre

    def body(g, carry):
        cre, cim = carry
        r0 = pl.multiple_of(g * SUBLANES, SUBLANES)
        re = h_sc[pl.ds(r0, SUBLANES), :nl]
        im = h_sc[pl.ds(r0, SUBLANES), nl:]
        for si, k in enumerate(_S5_STEPS):
            re, im = cfma(re, im, coef_ref[si, 0], coef_ref[si, 1],
                          pltpu.roll(re, k, axis=0), pltpu.roll(im, k, axis=0))
        last = len(_S5_STEPS)
        re, im = cfma(re, im, coef_ref[last, 0], coef_ref[last, 1],
                      jnp.broadcast_to(cre, re.shape), jnp.broadcast_to(cim, im.shape))
        h_sc[pl.ds(r0, SUBLANES), :nl] = re
        h_sc[pl.ds(r0, SUBLANES), nl:] = im
        return re[SUBLANES - 1:, :], im[SUBLANES - 1:, :]

    cre, cim = lax.fori_loop(0, ts // SUBLANES, body, (carry_sc[0:1, :nl], carry_sc[0:1, nl:]))
    carry_sc[0:1, :nl] = cre
    carry_sc[0:1, nl:] = cim

    y = _dot(h_sc[...].astype(BF16), wc_ref[...]) + d_ref[...] * u
    y = 0.5 * y * (1.0 + jnp.tanh(math.sqrt(2.0 / math.pi) * (y + 0.044715 * (y * y * y))))
    z = _dot(y.astype(BF16), gw_ref[...]) + gb_ref[...]
    o_ref[...] = y * jax.nn.sigmoid(z)


def _s5_weights(a_re, a_im, log_dt, b_re, b_im, c_re, c_im):
    g, p, c = b_re.shape
    ar, ai = a_re.astype(F32), a_im.astype(F32)
    dt = jnp.exp(log_dt.astype(F32))[:, None]

    def abar_pow(k):
        mag = jnp.exp(k * dt * ar)
        return (mag * jnp.cos(k * dt * ai)).reshape(-1), (mag * jnp.sin(k * dt * ai)).reshape(-1)

    abar_re, abar_im = jnp.exp(dt * ar) * jnp.cos(dt * ai), jnp.exp(dt * ar) * jnp.sin(dt * ai)
    nr, ni = abar_re - 1.0, abar_im
    den = ar * ar + ai * ai
    zr = (nr * ar + ni * ai) / den
    zi = (ni * ar - nr * ai) / den
    br, bi = b_re.astype(F32), b_im.astype(F32)
    bbar_re = zr[..., None] * br - zi[..., None] * bi
    bbar_im = zr[..., None] * bi + zi[..., None] * br
    eye = jnp.eye(g, dtype=F32)
    blockdiag_in = lambda m: jnp.einsum('gpc,gh->gchp', m, eye).reshape(g * c, g * p)
    wb = jnp.concatenate([blockdiag_in(bbar_re), blockdiag_in(bbar_im)], axis=1)
    blockdiag_out = lambda m: jnp.einsum('gcp,gh->gphc', m, eye).reshape(g * p, g * c)
    wc = jnp.concatenate([blockdiag_out(c_re.astype(F32)), -blockdiag_out(c_im.astype(F32))], axis=0)
    rows = jnp.arange(SUBLANES)[:, None]
    coefs = []
    for k in _S5_STEPS:
        pr, pi = abar_pow(k)
        coefs.append(jnp.stack([jnp.where(rows >= k, pr[None, :], 0.0),
                                jnp.where(rows >= k, pi[None, :], 0.0)]))
    per_row = [abar_pow(k + 1) for k in range(SUBLANES)]
    coefs.append(jnp.stack([jnp.stack([r for r, _ in per_row]), jnp.stack([i for _, i in per_row])]))
    return wb.astype(BF16), jnp.stack(coefs), wc.astype(BF16)


def _s5(u, wb, coef, wc, d_skip, glu_w, glu_b):
    bsz, s, w = u.shape
    ts = min(512, s)
    const = lambda shape: pl.BlockSpec(shape, lambda b, t: (0,) * len(shape))
    return pl.pallas_call(
        _s5_kernel,
        grid=(bsz, s // ts),
        in_specs=[pl.BlockSpec((None, ts, w), lambda b, t: (b, t, 0)),
                  const(wb.shape), const(coef.shape), const(wc.shape),
                  const((1, w)), const(glu_w.shape), const((1, w))],
        out_specs=pl.BlockSpec((None, ts, w), lambda b, t: (b, t, 0)),
        out_shape=jax.ShapeDtypeStruct((bsz, s, w), F32),
        scratch_shapes=[pltpu.VMEM((ts, 2 * S5_LANES), F32),
                        pltpu.VMEM((SUBLANES, 2 * S5_LANES), F32)],
        compiler_params=_params("arbitrary", "arbitrary"),
        name="s5",
    )(u, wb, coef, wc, d_skip.reshape(1, w), glu_w, glu_b.reshape(1, w))


def _hgrn_kernel(lb_ref, q_ref, f_ref, i_ref, o_ref, st_sc):
    th = q_ref.shape[0]
    c = HGRN_SUB
    w = BRANCH_W

    @pl.when(pl.program_id(1) == 0)
    def _():
        st_sc[...] = jnp.zeros_like(st_sc)

    lb = lb_ref[...]
    rowi = lax.broadcasted_iota(jnp.int32, (c, w), 0)
    same_head, head_ones = _same_head_ones()

    def body(ci, carry):
        r0 = pl.multiple_of(ci * c, c)
        q = q_ref[pl.ds(r0, c), :]
        qf = q * jax.nn.sigmoid(q) * (HEAD_DIM ** -0.5)
        f = lb + (1.0 - lb) * jax.nn.sigmoid(f_ref[pl.ds(r0, c), :])
        kk = 1.0 - f
        v = i_ref[pl.ds(r0, c), :]
        b = jnp.log(f)
        k = 1
        while k < c:
            b = b + jnp.where(rowi >= k, pltpu.roll(b, k, axis=0), 0.0)
            k *= 2
        e = jnp.concatenate(
            [jnp.exp(jnp.where(rowi >= s, b - b[s:s + 1, :], NEG)) * (qf * kk[s:s + 1, :])
             for s in range(c)], axis=0)
        wgt = _dot(e.astype(BF16), head_ones)
        o = jnp.zeros((c, w), F32)
        for s in range(c):
            o = o + wgt[s * c:(s + 1) * c, :] * v[s:s + 1, :]
        st = st_sc[...]
        o = o + _dot_nt((qf * jnp.exp(b)).astype(BF16), st.astype(BF16))
        b_last = b[c - 1:c, :]
        upd = _dot_tn(v.astype(BF16), (kk * jnp.exp(b_last - b)).astype(BF16))
        st_sc[...] = jnp.exp(b_last) * st + jnp.where(same_head, upd, 0.0)
        o_ref[pl.ds(r0, c), :] = o
        return carry

    lax.fori_loop(0, th // c, body, 0, unroll=2)


def _hgrn(lb, q, f, i):
    bsz, s, w = q.shape
    th = min(512, s)
    tile = pl.BlockSpec((None, th, w), lambda b, t: (b, t, 0))
    return pl.pallas_call(
        _hgrn_kernel,
        grid=(bsz, s // th),
        in_specs=[pl.BlockSpec((1, w), lambda b, t: (0, 0)), tile, tile, tile],
        out_specs=tile,
        out_shape=jax.ShapeDtypeStruct((bsz, s, w), F32),
        scratch_shapes=[pltpu.VMEM((w, w), F32)],
        compiler_params=_params("arbitrary", "arbitrary"),
        name="hgrn2",
    )(lb.reshape(1, w), q, f, i)


def _outproj_kernel(ya_ref, yb_ref, yc_ref, o1_ref, l1_ref, *rest, alpha):
    nres = len(_RESIDUE_DILS)
    res_refs = rest[:2 * nres]
    gates_ref, x_ref, w_ref, gain_ref, lng_ref, lnb_ref, out_ref, slab_sc = rest[2 * nres:]
    tm = x_ref.shape[0]
    halves = BRANCH_W // LANES

    def token_order(ref, dil):
        for rr in range(dil):
            for half in range(halves):
                c0 = rr * BRANCH_W + half * LANES
                slab_sc[half, pl.ds(rr, tm // dil, stride=dil), :] = ref[:, c0:c0 + LANES]
        return jnp.concatenate([slab_sc[half] for half in range(halves)], axis=1)

    outs, lses = [o1_ref[...]], [l1_ref[...]]
    for di, dil in enumerate(_RESIDUE_DILS):
        outs.append(token_order(res_refs[2 * di], dil))
        lses.append(token_order(res_refs[2 * di + 1], dil))
    lm = functools.reduce(jnp.maximum, lses)
    es = [jnp.exp(l - lm) for l in lses]
    yd = sum(e * o for e, o in zip(es, outs)) / sum(es)

    _, head_ones = _same_head_ones()
    normed = []
    for y in (ya_ref[...], yb_ref[...], yc_ref[...], yd):
        y2 = y * y
        hi = y2.astype(BF16)
        lo = (y2 - hi.astype(F32)).astype(BF16)
        ss = _dot(hi, head_ones) + _dot(lo, head_ones)
        normed.append(y * lax.rsqrt(ss * (1.0 / HEAD_DIM) + RMS_EPS))
    g = gates_ref[...]
    y = jnp.concatenate(normed, axis=-1) * gain_ref[...] * (g * jax.nn.sigmoid(g))
    r = alpha * x_ref[...] + _dot(y.astype(BF16), w_ref[...])
    mu = jnp.mean(r, axis=-1, keepdims=True)
    rc = r - mu
    var = jnp.mean(rc * rc, axis=-1, keepdims=True)
    out_ref[...] = rc * lax.rsqrt(var + LN_EPS) * lng_ref[...] + lnb_ref[...]


def _outproj(branches, residue_major, gates, xf, w, gain, ln_g, ln_b, alpha):
    t, d = xf.shape
    tm = OUT_TILE
    wide = pl.BlockSpec((tm, d), lambda i: (i, 0))
    narrow = pl.BlockSpec((tm, BRANCH_W), lambda i: (i, 0))
    vec = pl.BlockSpec((1, d), lambda i: (0, 0))
    res_specs, res_args = [], []
    for dil, (o, lse) in zip(_RESIDUE_DILS, residue_major):
        spec = pl.BlockSpec((tm // dil, dil * BRANCH_W), lambda i: (i, 0))
        res_specs += [spec, spec]
        res_args += [o, lse]
    return pl.pallas_call(
        functools.partial(_outproj_kernel, alpha=alpha),
        grid=(t // tm,),
        in_specs=[narrow] * 5 + res_specs + [wide, wide, pl.BlockSpec((d, d), lambda i: (0, 0)), vec, vec, vec],
        out_specs=wide,
        out_shape=jax.ShapeDtypeStruct((t, d), F32),
        scratch_shapes=[pltpu.VMEM((BRANCH_W // LANES, tm, LANES), F32)],
        compiler_params=_params("arbitrary"),
        name="outproj",
    )(*branches, *res_args, gates, xf, w, gain.reshape(1, d), ln_g.reshape(1, d), ln_b.reshape(1, d))


def kernel(x, w_in, rel_bias, s5_a_re, s5_a_im, s5_log_dt, s5_b_re, s5_b_im, s5_c_re, s5_c_im,
           s5_d, s5_glu_w, s5_glu_b, hgrn_lower, branch_gain, w_out, ln_g, ln_b):
    bsz, s, d = x.shape
    depth = w_in.shape[0]
    t = bsz * s
    w = BRANCH_W
    alpha = (2 * depth) ** 0.25
    p_lb = jax.nn.softmax(hgrn_lower.astype(F32), axis=0)
    lb_all = jnp.cumsum(p_lb, axis=0) - p_lb[0]
    tab = rel_bias.astype(F32)
    w_in_b = w_in.astype(BF16)
    w_out_b = w_out.astype(BF16)
    glu_w_b = s5_glu_w.astype(BF16)

    xf = x.reshape(t, d).astype(F32)
    for l in range(depth):
        wvt = w_in_b[l][:, _MOBA_V_SLOT * w:(_MOBA_V_SLOT + 1) * w].T
        plain, by_dil = _inproj(xf, w_in_b[l], wvt)
        qa, ka, vat, us, qc, fc, ic, qd, kd, vd, gates = plain
        seq = lambda a: a.reshape(bsz, s, w)
        ya = _moba(tab, seq(qa), seq(ka), vat.reshape(bsz, s // MOBA_BLOCK, w, MOBA_BLOCK))
        wb, coef, wc = _s5_weights(s5_a_re[l], s5_a_im[l], s5_log_dt[l], s5_b_re[l], s5_b_im[l],
                                   s5_c_re[l], s5_c_im[l])
        yb = _s5(seq(us), wb, coef, wc, s5_d[l].astype(F32), glu_w_b[l], s5_glu_b[l].astype(F32))
        yc = _hgrn(lb_all[l], seq(qc), seq(fc), seq(ic))
        o1, lse1 = _dilated(tab, qd, kd, vd, 1, bsz)
        residue_major = [_dilated(tab, *by_dil[dil], dil, bsz) for dil in _RESIDUE_DILS]
        flat = lambda a: a.reshape(t, w)
        xf = _outproj([flat(ya), flat(yb), flat(yc), o1, lse1], residue_major, gates, xf, w_out_b[l],
                      branch_gain[l].astype(F32), ln_g[l].astype(F32), ln_b[l].astype(F32), alpha)
    return xf.reshape(bsz, s, d).astype(x.dtype)
```

```python
import functools
import math

import numpy as np
import jax
import jax.numpy as jnp
from jax import lax
from jax.experimental import pallas as pl
from jax.experimental.pallas import tpu as pltpu

F32 = jnp.float32
BF16 = jnp.bfloat16

HEAD_DIM = 64
N_HEADS_BR = 4
BRANCH_W = N_HEADS_BR * HEAD_DIM
N_IN_SLOTS = 10
MOBA_BLOCK = 256
MOBA_TOPK = 3
S5_GROUP = 16
S5_GROUPS = BRANCH_W // S5_GROUP
S5_STATE = 64
S5_LANES = S5_GROUPS * S5_STATE
DIL_PAIRS = ((128, 1), (512, 4), (2048, 16))
DIL_BLOCK = 128
REL_BUCKETS = 32
REL_MAX_DIST = 2048
LN_EPS = 1e-5
RMS_EPS = 1e-6
NEG = -1e30
HGRN_SUB = 16
SUBLANES = 8
LANES = 128
BF16_ROWS = 16
V7X_VMEM_LIMIT = 56 * 1024 * 1024
LOG2E = math.log2(math.e)
LN2 = math.log(2.0)
QK_SCALE = HEAD_DIM ** -0.5 * LOG2E
IN_TILE = 512
OUT_TILE = 256


def _bucket_thresholds():
    max_exact = REL_BUCKETS // 2
    d = np.arange(0, 4 * REL_MAX_DIST, dtype=np.int64)
    large = max_exact + (np.log(np.maximum(d, 1) / max_exact) / math.log(REL_MAX_DIST / max_exact)
                         * (REL_BUCKETS - max_exact)).astype(np.int64)
    bucket = np.where(d < max_exact, d, np.minimum(large, REL_BUCKETS - 1))
    assert np.all(np.diff(bucket) >= 0)
    return [int(np.argmax(bucket >= j)) for j in range(REL_BUCKETS)]


_BUCKET_THR = _bucket_thresholds()


def _bias_from_dist(dist, tab_ref, head):
    val = jnp.full(dist.shape, tab_ref[0, head] * LOG2E, F32)
    for j in range(1, REL_BUCKETS):
        val = jnp.where(dist >= _BUCKET_THR[j], tab_ref[j, head] * LOG2E, val)
    return val


def _dot_nt(a, b):
    return lax.dot_general(a, b, (((1,), (1,)), ((), ())), preferred_element_type=F32)


def _dot_tn(a, b):
    return lax.dot_general(a, b, (((0,), (0,)), ((), ())), preferred_element_type=F32)


def _dot(a, b):
    return jnp.dot(a, b, preferred_element_type=F32)


def _lane_head(shape):
    return lax.shift_right_logical(lax.broadcasted_iota(jnp.int32, shape, 1), 6)


def _stack_heads(q):
    lh = _lane_head(q.shape)
    zero = jnp.zeros_like(q)
    return jnp.concatenate([jnp.where(lh == h, q, zero) for h in range(N_HEADS_BR)], axis=0)


def _merge_heads(o4, n):
    lh = _lane_head((n, BRANCH_W))
    out = jnp.zeros((n, BRANCH_W), F32)
    for h in range(N_HEADS_BR):
        out = jnp.where(lh == h, o4[h * n:(h + 1) * n, :], out)
    return out


def _same_head_ones():
    w = BRANCH_W
    same = (lax.shift_right_logical(lax.broadcasted_iota(jnp.int32, (w, w), 0), 6)
            == lax.shift_right_logical(lax.broadcasted_iota(jnp.int32, (w, w), 1), 6))
    return same, jnp.where(same, 1.0, 0.0).astype(BF16)


def _params(*sem):
    return pltpu.CompilerParams(dimension_semantics=sem, vmem_limit_bytes=V7X_VMEM_LIMIT)


_BF16_SLOTS = (0, 1, 2, 7, 8, 9)
_Q_SLOTS = (0, 7)
_MOBA_V_SLOT = 2
_DIL_SLOTS = (7, 8, 9)
_RESIDUE_DILS = tuple(dil for _, dil in DIL_PAIRS if dil > 1)


def _inproj_kernel(x_ref, w_ref, wvt_ref, *refs):
    n_plain = N_IN_SLOTS + 1
    out_refs = refs[:n_plain]
    res_refs = refs[n_plain:n_plain + len(_DIL_SLOTS) * len(_RESIDUE_DILS)]
    slab_sc = refs[-1]
    xb = x_ref[...].astype(BF16)
    tm = xb.shape[0]
    blk = MOBA_BLOCK
    for s in range(N_IN_SLOTS):
        if s == _MOBA_V_SLOT:
            vt = _dot_nt(wvt_ref[...], xb).astype(BF16)
            for j in range(tm // blk):
                out_refs[s][j] = vt[:, j * blk:(j + 1) * blk]
            continue
        r = _dot(xb, w_ref[:, s * BRANCH_W:(s + 1) * BRANCH_W])
        if s in _Q_SLOTS:
            r = r * QK_SCALE
        out_refs[s][...] = r.astype(out_refs[s].dtype)
        if s in _DIL_SLOTS:
            for half in range(BRANCH_W // LANES):
                slab_sc[half] = r[:, half * LANES:(half + 1) * LANES]
            for di, dil in enumerate(_RESIDUE_DILS):
                o_ref = res_refs[_DIL_SLOTS.index(s) * len(_RESIDUE_DILS) + di]
                for rr in range(dil):
                    for half in range(BRANCH_W // LANES):
                        c0 = rr * BRANCH_W + half * LANES
                        o_ref[:, c0:c0 + LANES] = slab_sc[half, pl.ds(rr, tm // dil, stride=dil), :].astype(BF16)
    g0 = N_IN_SLOTS * BRANCH_W
    out_refs[N_IN_SLOTS][...] = _dot(xb, w_ref[:, g0:])


def _inproj(xf, w, wvt):
    t, d = xf.shape
    tm = IN_TILE
    ncol = w.shape[1]
    blk = MOBA_BLOCK
    shapes = [jax.ShapeDtypeStruct((t, BRANCH_W), BF16 if s in _BF16_SLOTS else F32)
              for s in range(N_IN_SLOTS)]
    shapes.append(jax.ShapeDtypeStruct((t, ncol - N_IN_SLOTS * BRANCH_W), F32))
    out_specs = [pl.BlockSpec((tm, sh.shape[1]), lambda i: (i, 0)) for sh in shapes]
    shapes[_MOBA_V_SLOT] = jax.ShapeDtypeStruct((t // blk, BRANCH_W, blk), BF16)
    out_specs[_MOBA_V_SLOT] = pl.BlockSpec((tm // blk, BRANCH_W, blk), lambda i: (i, 0, 0))
    for _ in _DIL_SLOTS:
        for dil in _RESIDUE_DILS:
            shapes.append(jax.ShapeDtypeStruct((t // dil, dil * BRANCH_W), BF16))
            out_specs.append(pl.BlockSpec((tm // dil, dil * BRANCH_W), lambda i: (i, 0)))
    outs = pl.pallas_call(
        _inproj_kernel,
        grid=(t // tm,),
        in_specs=[pl.BlockSpec((tm, d), lambda i: (i, 0)),
                  pl.BlockSpec((d, ncol), lambda i: (0, 0)),
                  pl.BlockSpec((BRANCH_W, d), lambda i: (0, 0))],
        out_specs=out_specs,
        out_shape=shapes,
        scratch_shapes=[pltpu.VMEM((BRANCH_W // LANES, tm, LANES), F32)],
        compiler_params=_params("arbitrary"),
        name="inproj",
    )(xf, w, wvt)
    plain, res = outs[:N_IN_SLOTS + 1], outs[N_IN_SLOTS + 1:]
    nres = len(_RESIDUE_DILS)
    by_dil = {dil: [res[si * nres + di] for si in range(len(_DIL_SLOTS))]
              for di, dil in enumerate(_RESIDUE_DILS)}
    return plain, by_dil


_MOBA_BIAS_TILES = REL_MAX_DIST // MOBA_BLOCK + 2
_MASKED_OUT = 1e30


def _moba_kernel(tab_ref, q_ref, k_ref, vt_ref, o_ref,
                 bias_sc, kmean_sc, q4_sc, sel_sc, s0_sc, s1_sc, s2_sc, s3_sc,
                 mx0_sc, mx1_sc, mx2_sc, mx3_sc, acc_sc, m_sc, l_sc):
    b = pl.program_id(0)
    i = pl.program_id(1)
    blk = MOBA_BLOCK
    nkb = k_ref.shape[0] // blk
    nh = N_HEADS_BR
    hd = HEAD_DIM

    @pl.when((b == 0) & (i == 0))
    def _():
        key = lax.broadcasted_iota(jnp.int32, (blk, blk), 0)
        qry = lax.broadcasted_iota(jnp.int32, (blk, blk), 1)
        for m in range(_MOBA_BIAS_TILES):
            for h in range(nh):
                if m == _MOBA_BIAS_TILES - 1:
                    tile = jnp.full((blk, blk), tab_ref[REL_BUCKETS - 1, h] * LOG2E, F32)
                else:
                    tile = _bias_from_dist(jnp.maximum(m * blk + qry - key, 0), tab_ref, h)
                if m == 0:
                    tile = jnp.where(qry >= key, tile, NEG)
                bias_sc[m, :, h * blk:(h + 1) * blk] = tile.astype(BF16)

    @pl.when(i == 0)
    def _():
        for n in range(nkb):
            kb = k_ref[n * blk:(n + 1) * blk, :].astype(F32)
            kmean_sc[n:n + 1, :] = jnp.sum(kb, axis=0, keepdims=True) * (1.0 / blk)

    q4_sc[...] = _stack_heads(q_ref[...])
    q4 = q4_sc[...]

    km = kmean_sc[...]
    km_hi = km.astype(BF16)
    km_lo = (km - km_hi.astype(F32)).astype(BF16)
    gate = _dot_nt(km_hi, q4) + _dot_nt(km_lo, q4)
    blk_id = lax.broadcasted_iota(jnp.int32, gate.shape, 0)
    gate = jnp.where(blk_id < i, gate, -jnp.inf)
    sel = jnp.where(blk_id == i, 1.0, 0.0)
    for _ in range(MOBA_TOPK):
        mx = jnp.max(gate, axis=0, keepdims=True)
        idx = jnp.min(jnp.where(gate == mx, blk_id, nkb), axis=0, keepdims=True)
        pick = (blk_id == idx) & (mx > -jnp.inf)
        sel = jnp.where(pick, 1.0, sel)
        gate = jnp.where(pick, -jnp.inf, gate)
    sel_sc[...] = sel

    m_sc[...] = jnp.full(m_sc.shape, NEG, F32)
    l_sc[...] = jnp.zeros(l_sc.shape, F32)
    acc_sc[...] = jnp.zeros(acc_sc.shape, F32)
    ones_rows = jnp.ones((BF16_ROWS, blk), BF16)

    def scores(n, dst_sc, dst_max_sc):
        k0 = pl.multiple_of(jnp.minimum(n, nkb - 1) * blk, blk)
        s = (_dot_nt(k_ref[pl.ds(k0, blk), :], q4_sc[...]).astype(BF16)
             + bias_sc[jnp.clip(i - n, 0, _MOBA_BIAS_TILES - 1)])
        dst_sc[...] = s
        dst_max_sc[...] = jnp.max(s, axis=0, keepdims=True).astype(F32)

    def visit(n, src_sc, src_max_sc):
        nc = jnp.minimum(n, nkb - 1)
        chosen = sel_sc[pl.ds(nc, 1), :] > 0.0
        m_old = m_sc[...]
        m_new = jnp.maximum(m_old, jnp.where(chosen, src_max_sc[...], NEG))
        alpha = jnp.exp2(m_old - m_new)
        pb = jnp.exp2(src_sc[...] - jnp.where(chosen, m_new, _MASKED_OUT).astype(BF16))
        vt = vt_ref[nc]
        pv = jnp.concatenate(
            [_dot(jnp.concatenate([vt[h * hd:(h + 1) * hd, :], ones_rows], axis=0),
                  pb[:, h * blk:(h + 1) * blk]) for h in range(nh)], axis=1)
        acc_sc[...] = alpha * acc_sc[...] + pv[:hd]
        l_sc[...] = alpha * l_sc[...] + pv[hd:hd + 1]
        m_sc[...] = m_new

    bufs = ((s0_sc, mx0_sc), (s1_sc, mx1_sc), (s2_sc, mx2_sc), (s3_sc, mx3_sc))
    scores(0, *bufs[0])
    scores(1, *bufs[1])
    n_quads = (i + 1) // 4

    def quad_body(j, carry):
        n = 4 * j
        scores(n + 2, *bufs[2])
        visit(n, *bufs[0])
        scores(n + 3, *bufs[3])
        visit(n + 1, *bufs[1])
        scores(n + 4, *bufs[0])
        visit(n + 2, *bufs[2])
        scores(n + 5, *bufs[1])
        visit(n + 3, *bufs[3])
        return carry

    def pair_body(j, carry):
        n = 4 * n_quads + 2 * j
        visit(n, *bufs[0])
        scores(n + 2, *bufs[0])
        visit(n + 1, *bufs[1])
        scores(n + 3, *bufs[1])
        return carry

    lax.fori_loop(0, n_quads, quad_body, 0)
    lax.fori_loop(0, (i + 2 - 4 * n_quads) // 2, pair_body, 0)

    ot = acc_sc[...] / l_sc[...]
    drow = lax.broadcasted_iota(jnp.int32, (hd, BRANCH_W), 0)
    lane = lax.broadcasted_iota(jnp.int32, (hd, BRANCH_W), 1)
    out = jnp.zeros((blk, BRANCH_W), F32)
    for h in range(nh):
        place = jnp.where(lane == drow + h * hd, 1.0, 0.0).astype(BF16)
        piece = ot[:, h * blk:(h + 1) * blk]
        hi = piece.astype(BF16)
        lo = (piece - hi.astype(F32)).astype(BF16)
        out = out + _dot_tn(hi, place) + _dot_tn(lo, place)
    o_ref[...] = out


def _moba(tab, q, k, vt):
    bsz, s, w = q.shape
    blk = MOBA_BLOCK
    nkb = s // blk
    nh = N_HEADS_BR
    return pl.pallas_call(
        _moba_kernel,
        grid=(bsz, nkb),
        in_specs=[pl.BlockSpec(memory_space=pltpu.SMEM),
                  pl.BlockSpec((None, blk, w), lambda b, i: (b, i, 0)),
                  pl.BlockSpec((None, s, w), lambda b, i: (b, 0, 0)),
                  pl.BlockSpec((None, nkb, w, blk), lambda b, i: (b, 0, 0, 0))],
        out_specs=pl.BlockSpec((None, blk, w), lambda b, i: (b, i, 0)),
        out_shape=jax.ShapeDtypeStruct((bsz, s, w), F32),
        scratch_shapes=[pltpu.VMEM((_MOBA_BIAS_TILES, blk, nh * blk), BF16),
                        pltpu.VMEM((nkb, w), F32),
                        pltpu.VMEM((nh * blk, w), BF16),
                        pltpu.VMEM((nkb, nh * blk), F32),
                        *[pltpu.VMEM((blk, nh * blk), BF16) for _ in range(4)],
                        *[pltpu.VMEM((1, nh * blk), F32) for _ in range(4)],
                        pltpu.VMEM((HEAD_DIM, nh * blk), F32),
                        pltpu.VMEM((1, nh * blk), F32),
                        pltpu.VMEM((1, nh * blk), F32)],
        compiler_params=_params("arbitrary", "arbitrary"),
        name="moba",
    )(tab, q, k, vt)


def _dil_kernel(tab_ref, q_ref, kc_ref, kp_ref, vc_ref, vp_ref, o_ref, lse_ref, bias_sc, *, dil, tq):
    nb = DIL_BLOCK
    nh = N_HEADS_BR
    t = pl.program_id(2)
    first = (pl.program_id(0) == 0) & (pl.program_id(1) == 0) & (t == 0)

    @pl.when(first)
    def _():
        row = lax.broadcasted_iota(jnp.int32, (nb, 2 * nb), 0)
        col = lax.broadcasted_iota(jnp.int32, (nb, 2 * nb), 1)
        dist_sub = nb + row - col
        band = (dist_sub >= 0) & (dist_sub <= nb)
        for h in range(nh):
            tile = _bias_from_dist(jnp.maximum(dist_sub, 0) * dil, tab_ref, nh + h)
            bias_sc[h * nb:(h + 1) * nb, :] = jnp.where(band, tile, NEG)

    for jb in range(tq // nb):
        lo, hi = jb * nb, (jb + 1) * nb
        q4 = _stack_heads(q_ref[lo:hi, :])
        if jb == 0:
            kprev, vprev = kp_ref[tq - nb:tq, :], vp_ref[tq - nb:tq, :]
        else:
            kprev, vprev = kc_ref[lo - nb:lo, :], vc_ref[lo - nb:lo, :]
        keys = jnp.concatenate([kprev, kc_ref[lo:hi, :]], axis=0)
        vals = jnp.concatenate([vprev, vc_ref[lo:hi, :]], axis=0)
        s = _dot_nt(q4, keys) + bias_sc[...]
        if jb == 0:
            col = lax.broadcasted_iota(jnp.int32, s.shape, 1)
            s = jnp.where((col >= nb) | (t > 0), s, NEG)
        m = jnp.max(s, axis=-1, keepdims=True)
        p = jnp.exp2(s - m)
        l = jnp.sum(p, axis=-1, keepdims=True)
        o4 = _dot(p.astype(BF16), vals) / l
        lse4 = jnp.broadcast_to((m + jnp.log2(l)) * LN2, o4.shape)
        o_ref[lo:hi, :] = _merge_heads(o4, nb)
        lse_ref[lo:hi, :] = _merge_heads(lse4, nb)


def _dilated(tab, q, k, v, dil, bsz):
    rows, width = q.shape
    w = BRANCH_W
    sub_len = rows // bsz
    tq = min(512, sub_len)
    nt = sub_len // tq
    view = lambda a: a.reshape(bsz, sub_len, width)
    cur = pl.BlockSpec((None, tq, w), lambda b, r, t: (b, t, r))
    prev = pl.BlockSpec((None, tq, w), lambda b, r, t: (b, jnp.maximum(t - 1, 0), r))
    out_shape = jax.ShapeDtypeStruct((bsz, sub_len, width), F32)
    o, lse = pl.pallas_call(
        functools.partial(_dil_kernel, dil=dil, tq=tq),
        grid=(bsz, dil, nt),
        in_specs=[pl.BlockSpec(memory_space=pltpu.SMEM), cur, cur, prev, cur, prev],
        out_specs=[cur, cur],
        out_shape=[out_shape, out_shape],
        scratch_shapes=[pltpu.VMEM((N_HEADS_BR * DIL_BLOCK, 2 * DIL_BLOCK), F32)],
        compiler_params=_params("arbitrary", "arbitrary", "arbitrary"),
        name=f"dilated{dil}",
    )(tab, view(q), view(k), view(k), view(v), view(v))
    return o.reshape(rows, width), lse.reshape(rows, width)


_S5_STEPS = (1, 2, 4)


def _s5_kernel(u_ref, wb_ref, coef_ref, wc_ref, d_ref, gw_ref, gb_ref, o_ref, h_sc, carry_sc):
    ts = u_ref.shape[0]
    nl = S5_LANES

    @pl.when(pl.program_id(1) == 0)
    def _():
        carry_sc[...] = jnp.zeros_like(carry_sc)

    u = u_ref[...]
    h_sc[...] = _dot(u.astype(BF16), wb_ref[...])

    def cfma(re, im, cr, ci, sre, sim):
        return re + cr * sre - ci * sim, im + cr * sim + ci * sre

    def body(g, carry):
        cre, cim = carry
        r0 = pl.multiple_of(g * SUBLANES, SUBLANES)
        re = h_sc[pl.ds(r0, SUBLANES), :nl]
        im = h_sc[pl.ds(r0, SUBLANES), nl:]
        for si, k in enumerate(_S5_STEPS):
            re, im = cfma(re, im, coef_ref[si, 0], coef_ref[si, 1],
                          pltpu.roll(re, k, axis=0), pltpu.roll(im, k, axis=0))
        last = len(_S5_STEPS)
        re, im = cfma(re, im, coef_ref[last, 0], coef_ref[last, 1],
                      jnp.broadcast_to(cre, re.shape), jnp.broadcast_to(cim, im.shape))
        h_sc[pl.ds(r0, SUBLANES), :nl] = re
        h_sc[pl.ds(r0, SUBLANES), nl:] = im
        return re[SUBLANES - 1:, :], im[SUBLANES - 1:, :]

    cre, cim = lax.fori_loop(0, ts // SUBLANES, body, (carry_sc[0:1, :nl], carry_sc[0:1, nl:]))
    carry_sc[0:1, :nl] = cre
    carry_sc[0:1, nl:] = cim

    y = _dot(h_sc[...].astype(BF16), wc_ref[...]) + d_ref[...] * u
    y = 0.5 * y * (1.0 + jnp.tanh(math.sqrt(2.0 / math.pi) * (y + 0.044715 * (y * y * y))))
    z = _dot(y.astype(BF16), gw_ref[...]) + gb_ref[...]
    o_ref[...] = y * jax.nn.sigmoid(z)


def _s5_weights(a_re, a_im, log_dt, b_re, b_im, c_re, c_im):
    g, p, c = b_re.shape
    ar, ai = a_re.astype(F32), a_im.astype(F32)
    dt = jnp.exp(log_dt.astype(F32))[:, None]

    def abar_pow(k):
        mag = jnp.exp(k * dt * ar)
        return (mag * jnp.cos(k * dt * ai)).reshape(-1), (mag * jnp.sin(k * dt * ai)).reshape(-1)

    abar_re, abar_im = jnp.exp(dt * ar) * jnp.cos(dt * ai), jnp.exp(dt * ar) * jnp.sin(dt * ai)
    nr, ni = abar_re - 1.0, abar_im
    den = ar * ar + ai * ai
    zr = (nr * ar + ni * ai) / den
    zi = (ni * ar - nr * ai) / den
    br, bi = b_re.astype(F32), b_im.astype(F32)
    bbar_re = zr[..., None] * br - zi[..., None] * bi
    bbar_im = zr[..., None] * bi + zi[..., None] * br
    eye = jnp.eye(g, dtype=F32)
    blockdiag_in = lambda m: jnp.einsum('gpc,gh->gchp', m, eye).reshape(g * c, g * p)
    wb = jnp.concatenate([blockdiag_in(bbar_re), blockdiag_in(bbar_im)], axis=1)
    blockdiag_out = lambda m: jnp.einsum('gcp,gh->gphc', m, eye).reshape(g * p, g * c)
    wc = jnp.concatenate([blockdiag_out(c_re.astype(F32)), -blockdiag_out(c_im.astype(F32))], axis=0)
    rows = jnp.arange(SUBLANES)[:, None]
    coefs = []
    for k in _S5_STEPS:
        pr, pi = abar_pow(k)
        coefs.append(jnp.stack([jnp.where(rows >= k, pr[None, :], 0.0),
                                jnp.where(rows >= k, pi[None, :], 0.0)]))
    per_row = [abar_pow(k + 1) for k in range(SUBLANES)]
    coefs.append(jnp.stack([jnp.stack([r for r, _ in per_row]), jnp.stack([i for _, i in per_row])]))
    return wb.astype(BF16), jnp.stack(coefs), wc.astype(BF16)


def _s5(u, wb, coef, wc, d_skip, glu_w, glu_b):
    bsz, s, w = u.shape
    ts = min(512, s)
    const = lambda shape: pl.BlockSpec(shape, lambda b, t: (0,) * len(shape))
    return pl.pallas_call(
        _s5_kernel,
        grid=(bsz, s // ts),
        in_specs=[pl.BlockSpec((None, ts, w), lambda b, t: (b, t, 0)),
                  const(wb.shape), const(coef.shape), const(wc.shape),
                  const((1, w)), const(glu_w.shape), const((1, w))],
        out_specs=pl.BlockSpec((None, ts, w), lambda b, t: (b, t, 0)),
        out_shape=jax.ShapeDtypeStruct((bsz, s, w), F32),
        scratch_shapes=[pltpu.VMEM((ts, 2 * S5_LANES), F32),
                        pltpu.VMEM((SUBLANES, 2 * S5_LANES), F32)],
        compiler_params=_params("arbitrary", "arbitrary"),
        name="s5",
    )(u, wb, coef, wc, d_skip.reshape(1, w), glu_w, glu_b.reshape(1, w))


def _hgrn_kernel(lb_ref, q_ref, f_ref, i_ref, o_ref, st_sc):
    th = q_ref.shape[0]
    c = HGRN_SUB
    w = BRANCH_W

    @pl.when(pl.program_id(1) == 0)
    def _():
        st_sc[...] = jnp.zeros_like(st_sc)

    lb = lb_ref[...]
    rowi = lax.broadcasted_iota(jnp.int32, (c, w), 0)
    same_head, head_ones = _same_head_ones()

    def body(ci, carry):
        r0 = pl.multiple_of(ci * c, c)
        q = q_ref[pl.ds(r0, c), :]
        qf = q * jax.nn.sigmoid(q) * (HEAD_DIM ** -0.5)
        f = lb + (1.0 - lb) * jax.nn.sigmoid(f_ref[pl.ds(r0, c), :])
        kk = 1.0 - f
        v = i_ref[pl.ds(r0, c), :]
        b = jnp.log(f)
        k = 1
        while k < c:
            b = b + jnp.where(rowi >= k, pltpu.roll(b, k, axis=0), 0.0)
            k *= 2
        e = jnp.concatenate(
            [jnp.exp(jnp.where(rowi >= s, b - b[s:s + 1, :], NEG)) * (qf * kk[s:s + 1, :])
             for s in range(c)], axis=0)
        wgt = _dot(e.astype(BF16), head_ones)
        o = jnp.zeros((c, w), F32)
        for s in range(c):
            o = o + wgt[s * c:(s + 1) * c, :] * v[s:s + 1, :]
        st = st_sc[...]
        o = o + _dot_nt((qf * jnp.exp(b)).astype(BF16), st.astype(BF16))
        b_last = b[c - 1:c, :]
        upd = _dot_tn(v.astype(BF16), (kk * jnp.exp(b_last - b)).astype(BF16))
        st_sc[...] = jnp.exp(b_last) * st + jnp.where(same_head, upd, 0.0)
        o_ref[pl.ds(r0, c), :] = o
        return carry

    lax.fori_loop(0, th // c, body, 0, unroll=2)


def _hgrn(lb, q, f, i):
    bsz, s, w = q.shape
    th = min(512, s)
    tile = pl.BlockSpec((None, th, w), lambda b, t: (b, t, 0))
    return pl.pallas_call(
        _hgrn_kernel,
        grid=(bsz, s // th),
        in_specs=[pl.BlockSpec((1, w), lambda b, t: (0, 0)), tile, tile, tile],
        out_specs=tile,
        out_shape=jax.ShapeDtypeStruct((bsz, s, w), F32),
        scratch_shapes=[pltpu.VMEM((w, w), F32)],
        compiler_params=_params("arbitrary", "arbitrary"),
        name="hgrn2",
    )(lb.reshape(1, w), q, f, i)


def _outproj_kernel(ya_ref, yb_ref, yc_ref, o1_ref, l1_ref, *rest, alpha):
    nres = len(_RESIDUE_DILS)
    res_refs = rest[:2 * nres]
    gates_ref, x_ref, w_ref, gain_ref, lng_ref, lnb_ref, out_ref, slab_sc = rest[2 * nres:]
    tm = x_ref.shape[0]
    halves = BRANCH_W // LANES

    def token_order(ref, dil):
        for rr in range(dil):
            for half in range(halves):
                c0 = rr * BRANCH_W + half * LANES
                slab_sc[half, pl.ds(rr, tm // dil, stride=dil), :] = ref[:, c0:c0 + LANES]
        return jnp.concatenate([slab_sc[half] for half in range(halves)], axis=1)

    outs, lses = [o1_ref[...]], [l1_ref[...]]
    for di, dil in enumerate(_RESIDUE_DILS):
        outs.append(token_order(res_refs[2 * di], dil))
        lses.append(token_order(res_refs[2 * di + 1], dil))
    lm = functools.reduce(jnp.maximum, lses)
    es = [jnp.exp(l - lm) for l in lses]
    yd = sum(e * o for e, o in zip(es, outs)) / sum(es)

    _, head_ones = _same_head_ones()
    normed = []
    for y in (ya_ref[...], yb_ref[...], yc_ref[...], yd):
        y2 = y * y
        hi = y2.astype(BF16)
        lo = (y2 - hi.astype(F32)).astype(BF16)
        ss = _dot(hi, head_ones) + _dot(lo, head_ones)
        normed.append(y * lax.rsqrt(ss * (1.0 / HEAD_DIM) + RMS_EPS))
    g = gates_ref[...]
    y = jnp.concatenate(normed, axis=-1) * gain_ref[...] * (g * jax.nn.sigmoid(g))
    r = alpha * x_ref[...] + _dot(y.astype(BF16), w_ref[...])
    mu = jnp.mean(r, axis=-1, keepdims=True)
    rc = r - mu
    var = jnp.mean(rc * rc, axis=-1, keepdims=True)
    out_ref[...] = rc * lax.rsqrt(var + LN_EPS) * lng_ref[...] + lnb_ref[...]


def _outproj(branches, residue_major, gates, xf, w, gain, ln_g, ln_b, alpha):
    t, d = xf.shape
    tm = OUT_TILE
    wide = pl.BlockSpec((tm, d), lambda i: (i, 0))
    narrow = pl.BlockSpec((tm, BRANCH_W), lambda i: (i, 0))
    vec = pl.BlockSpec((1, d), lambda i: (0, 0))
    res_specs, res_args = [], []
    for dil, (o, lse) in zip(_RESIDUE_DILS, residue_major):
        spec = pl.BlockSpec((tm // dil, dil * BRANCH_W), lambda i: (i, 0))
        res_specs += [spec, spec]
        res_args += [o, lse]
    return pl.pallas_call(
        functools.partial(_outproj_kernel, alpha=alpha),
        grid=(t // tm,),
        in_specs=[narrow] * 5 + res_specs + [wide, wide, pl.BlockSpec((d, d), lambda i: (0, 0)), vec, vec, vec],
        out_specs=wide,
        out_shape=jax.ShapeDtypeStruct((t, d), F32),
        scratch_shapes=[pltpu.VMEM((BRANCH_W // LANES, tm, LANES), F32)],
        compiler_params=_params("arbitrary"),
        name="outproj",
    )(*branches, *res_args, gates, xf, w, gain.reshape(1, d), ln_g.reshape(1, d), ln_b.reshape(1, d))


def kernel(x, w_in, rel_bias, s5_a_re, s5_a_im, s5_log_dt, s5_b_re, s5_b_im, s5_c_re, s5_c_im,
           s5_d, s5_glu_w, s5_glu_b, hgrn_lower, branch_gain, w_out, ln_g, ln_b):
    bsz, s, d = x.shape
    depth = w_in.shape[0]
    t = bsz * s
    w = BRANCH_W
    alpha = (2 * depth) ** 0.25
    p_lb = jax.nn.softmax(hgrn_lower.astype(F32), axis=0)
    lb_all = jnp.cumsum(p_lb, axis=0) - p_lb[0]
    tab = rel_bias.astype(F32)
    w_in_b = w_in.astype(BF16)
    w_out_b = w_out.astype(BF16)
    glu_w_b = s5_glu_w.astype(BF16)

    xf = x.reshape(t, d).astype(F32)
    for l in range(depth):
        wvt = w_in_b[l][:, _MOBA_V_SLOT * w:(_MOBA_V_SLOT + 1) * w].T
        plain, by_dil = _inproj(xf, w_in_b[l], wvt)
        qa, ka, vat, us, qc, fc, ic, qd, kd, vd, gates = plain
        seq = lambda a: a.reshape(bsz, s, w)
        ya = _moba(tab, seq(qa), seq(ka), vat.reshape(bsz, s // MOBA_BLOCK, w, MOBA_BLOCK))
        wb, coef, wc = _s5_weights(s5_a_re[l], s5_a_im[l], s5_log_dt[l], s5_b_re[l], s5_b_im[l],
                                   s5_c_re[l], s5_c_im[l])
        yb = _s5(seq(us), wb, coef, wc, s5_d[l].astype(F32), glu_w_b[l], s5_glu_b[l].astype(F32))
        yc = _hgrn(lb_all[l], seq(qc), seq(fc), seq(ic))
        o1, lse1 = _dilated(tab, qd, kd, vd, 1, bsz)
        residue_major = [_dilated(tab, *by_dil[dil], dil, bsz) for dil in _RESIDUE_DILS]
        flat = lambda a: a.reshape(t, w)
        xf = _outproj([flat(ya), flat(yb), flat(yc), o1, lse1], residue_major, gates, xf, w_out_b[l],
                      branch_gain[l].astype(F32), ln_g[l].astype(F32), ln_b[l].astype(F32), alpha)
    return xf.reshape(bsz, s, d).astype(x.dtype)
```

```python
import functools
import math

import numpy as np
import jax
import jax.numpy as jnp
from jax import lax
from jax.experimental import pallas as pl
from jax.experimental.pallas import tpu as pltpu

F32 = jnp.float32
BF16 = jnp.bfloat16

HEAD_DIM = 64
N_HEADS_BR = 4
BRANCH_W = N_HEADS_BR * HEAD_DIM
N_IN_SLOTS = 10
MOBA_BLOCK = 256
MOBA_TOPK = 3
S5_GROUP = 16
S5_GROUPS = BRANCH_W // S5_GROUP
S5_STATE = 64
S5_LANES = S5_GROUPS * S5_STATE
DIL_PAIRS = ((128, 1), (512, 4), (2048, 16))
DIL_BLOCK = 128
REL_BUCKETS = 32
REL_MAX_DIST = 2048
LN_EPS = 1e-5
RMS_EPS = 1e-6
NEG = -1e30
HGRN_SUB = 16
SUBLANES = 8
LANES = 128
BF16_ROWS = 16
V7X_VMEM_LIMIT = 56 * 1024 * 1024
LOG2E = math.log2(math.e)
LN2 = math.log(2.0)
QK_SCALE = HEAD_DIM ** -0.5 * LOG2E
IN_TILE = 512
OUT_TILE = 256


def _bucket_thresholds():
    max_exact = REL_BUCKETS // 2
    d = np.arange(0, 4 * REL_MAX_DIST, dtype=np.int64)
    large = max_exact + (np.log(np.maximum(d, 1) / max_exact) / math.log(REL_MAX_DIST / max_exact)
                         * (REL_BUCKETS - max_exact)).astype(np.int64)
    bucket = np.where(d < max_exact, d, np.minimum(large, REL_BUCKETS - 1))
    assert np.all(np.diff(bucket) >= 0)
    return [int(np.argmax(bucket >= j)) for j in range(REL_BUCKETS)]


_BUCKET_THR = _bucket_thresholds()


def _bias_from_dist(dist, tab_ref, head):
    val = jnp.full(dist.shape, tab_ref[0, head] * LOG2E, F32)
    for j in range(1, REL_BUCKETS):
        val = jnp.where(dist >= _BUCKET_THR[j], tab_ref[j, head] * LOG2E, val)
    return val


def _dot_nt(a, b):
    return lax.dot_general(a, b, (((1,), (1,)), ((), ())), preferred_element_type=F32)


def _dot_tn(a, b):
    return lax.dot_general(a, b, (((0,), (0,)), ((), ())), preferred_element_type=F32)


def _dot(a, b):
    return jnp.dot(a, b, preferred_element_type=F32)


def _lane_head(shape):
    return lax.shift_right_logical(lax.broadcasted_iota(jnp.int32, shape, 1), 6)


def _stack_heads(q):
    lh = _lane_head(q.shape)
    zero = jnp.zeros_like(q)
    return jnp.concatenate([jnp.where(lh == h, q, zero) for h in range(N_HEADS_BR)], axis=0)


def _merge_heads(o4, n):
    lh = _lane_head((n, BRANCH_W))
    out = jnp.zeros((n, BRANCH_W), F32)
    for h in range(N_HEADS_BR):
        out = jnp.where(lh == h, o4[h * n:(h + 1) * n, :], out)
    return out


def _same_head_ones():
    w = BRANCH_W
    same = (lax.shift_right_logical(lax.broadcasted_iota(jnp.int32, (w, w), 0), 6)
            == lax.shift_right_logical(lax.broadcasted_iota(jnp.int32, (w, w), 1), 6))
    return same, jnp.where(same, 1.0, 0.0).astype(BF16)


def _params(*sem):
    return pltpu.CompilerParams(dimension_semantics=sem, vmem_limit_bytes=V7X_VMEM_LIMIT)


_BF16_SLOTS = (0, 1, 2, 7, 8, 9)
_Q_SLOTS = (0, 7)
_MOBA_T_SLOTS = (0, 2)
_DIL_SLOTS = (7, 8, 9)
_RESIDUE_DILS = tuple(dil for _, dil in DIL_PAIRS if dil > 1)


def _inproj_kernel(x_ref, w_ref, wvt_ref, *refs):
    n_plain = N_IN_SLOTS + 1
    out_refs = refs[:n_plain]
    res_refs = refs[n_plain:n_plain + len(_DIL_SLOTS) * len(_RESIDUE_DILS)]
    slab_sc = refs[-1]
    xb = x_ref[...].astype(BF16)
    tm = xb.shape[0]
    blk = MOBA_BLOCK
    for s in range(N_IN_SLOTS):
        if s in _MOBA_T_SLOTS:
            ti = _MOBA_T_SLOTS.index(s)
            rt = _dot_nt(wvt_ref[ti * BRANCH_W:(ti + 1) * BRANCH_W, :], xb)
            if s in _Q_SLOTS:
                rt = rt * QK_SCALE
            rt = rt.astype(BF16)
            for j in range(tm // blk):
                out_refs[s][j] = rt[:, j * blk:(j + 1) * blk]
            continue
        r = _dot(xb, w_ref[:, s * BRANCH_W:(s + 1) * BRANCH_W])
        if s in _Q_SLOTS:
            r = r * QK_SCALE
        out_refs[s][...] = r.astype(out_refs[s].dtype)
        if s in _DIL_SLOTS:
            for half in range(BRANCH_W // LANES):
                slab_sc[half] = r[:, half * LANES:(half + 1) * LANES]
            for di, dil in enumerate(_RESIDUE_DILS):
                o_ref = res_refs[_DIL_SLOTS.index(s) * len(_RESIDUE_DILS) + di]
                for rr in range(dil):
                    for half in range(BRANCH_W // LANES):
                        c0 = rr * BRANCH_W + half * LANES
                        o_ref[:, c0:c0 + LANES] = slab_sc[half, pl.ds(rr, tm // dil, stride=dil), :].astype(BF16)
    g0 = N_IN_SLOTS * BRANCH_W
    out_refs[N_IN_SLOTS][...] = _dot(xb, w_ref[:, g0:])


def _inproj(xf, w, wvt):
    t, d = xf.shape
    tm = IN_TILE
    ncol = w.shape[1]
    blk = MOBA_BLOCK
    shapes = [jax.ShapeDtypeStruct((t, BRANCH_W), BF16 if s in _BF16_SLOTS else F32)
              for s in range(N_IN_SLOTS)]
    shapes.append(jax.ShapeDtypeStruct((t, ncol - N_IN_SLOTS * BRANCH_W), F32))
    out_specs = [pl.BlockSpec((tm, sh.shape[1]), lambda i: (i, 0)) for sh in shapes]
    for s in _MOBA_T_SLOTS:
        shapes[s] = jax.ShapeDtypeStruct((t // blk, BRANCH_W, blk), BF16)
        out_specs[s] = pl.BlockSpec((tm // blk, BRANCH_W, blk), lambda i: (i, 0, 0))
    for _ in _DIL_SLOTS:
        for dil in _RESIDUE_DILS:
            shapes.append(jax.ShapeDtypeStruct((t // dil, dil * BRANCH_W), BF16))
            out_specs.append(pl.BlockSpec((tm // dil, dil * BRANCH_W), lambda i: (i, 0)))
    outs = pl.pallas_call(
        _inproj_kernel,
        grid=(t // tm,),
        in_specs=[pl.BlockSpec((tm, d), lambda i: (i, 0)),
                  pl.BlockSpec((d, ncol), lambda i: (0, 0)),
                  pl.BlockSpec((len(_MOBA_T_SLOTS) * BRANCH_W, d), lambda i: (0, 0))],
        out_specs=out_specs,
        out_shape=shapes,
        scratch_shapes=[pltpu.VMEM((BRANCH_W // LANES, tm, LANES), F32)],
        compiler_params=_params("arbitrary"),
        name="inproj",
    )(xf, w, wvt)
    plain, res = outs[:N_IN_SLOTS + 1], outs[N_IN_SLOTS + 1:]
    nres = len(_RESIDUE_DILS)
    by_dil = {dil: [res[si * nres + di] for si in range(len(_DIL_SLOTS))]
              for di, dil in enumerate(_RESIDUE_DILS)}
    return plain, by_dil


_MOBA_BIAS_TILES = REL_MAX_DIST // MOBA_BLOCK + 2
_MASKED_OUT = 1e30


def _moba_kernel(tab_ref, q_ref, k_ref, vt_ref, o_ref,
                 bias_sc, kmean_sc, q4_sc, sel_sc, s0_sc, s1_sc, s2_sc, s3_sc,
                 mx0_sc, mx1_sc, mx2_sc, mx3_sc, acc_sc, m_sc, l_sc):
    b = pl.program_id(0)
    i = pl.program_id(1)
    blk = MOBA_BLOCK
    nkb = k_ref.shape[0] // blk
    nh = N_HEADS_BR
    hd = HEAD_DIM

    @pl.when((b == 0) & (i == 0))
    def _():
        key = lax.broadcasted_iota(jnp.int32, (blk, blk), 0)
        qry = lax.broadcasted_iota(jnp.int32, (blk, blk), 1)
        for m in range(_MOBA_BIAS_TILES):
            for h in range(nh):
                if m == _MOBA_BIAS_TILES - 1:
                    tile = jnp.full((blk, blk), tab_ref[REL_BUCKETS - 1, h] * LOG2E, F32)
                else:
                    tile = _bias_from_dist(jnp.maximum(m * blk + qry - key, 0), tab_ref, h)
                if m == 0:
                    tile = jnp.where(qry >= key, tile, NEG)
                bias_sc[m, :, h * blk:(h + 1) * blk] = tile.astype(BF16)

    @pl.when(i == 0)
    def _():
        for n in range(nkb):
            kb = k_ref[n * blk:(n + 1) * blk, :].astype(F32)
            kmean_sc[n:n + 1, :] = jnp.sum(kb, axis=0, keepdims=True) * (1.0 / blk)

    qt = q_ref[...]
    row_head = lax.shift_right_logical(lax.broadcasted_iota(jnp.int32, qt.shape, 0), 6)
    q4_sc[...] = jnp.concatenate(
        [jnp.where(row_head == h, qt, jnp.zeros_like(qt)) for h in range(nh)], axis=1)
    q4 = q4_sc[...]

    km = kmean_sc[...]
    km_hi = km.astype(BF16)
    km_lo = (km - km_hi.astype(F32)).astype(BF16)
    gate = _dot(km_hi, q4) + _dot(km_lo, q4)
    blk_id = lax.broadcasted_iota(jnp.int32, gate.shape, 0)
    gate = jnp.where(blk_id < i, gate, -jnp.inf)
    sel = jnp.where(blk_id == i, 1.0, 0.0)
    for _ in range(MOBA_TOPK):
        mx = jnp.max(gate, axis=0, keepdims=True)
        idx = jnp.min(jnp.where(gate == mx, blk_id, nkb), axis=0, keepdims=True)
        pick = (blk_id == idx) & (mx > -jnp.inf)
        sel = jnp.where(pick, 1.0, sel)
        gate = jnp.where(pick, -jnp.inf, gate)
    sel_sc[...] = sel

    m_sc[...] = jnp.full(m_sc.shape, NEG, F32)
    l_sc[...] = jnp.zeros(l_sc.shape, F32)
    acc_sc[...] = jnp.zeros(acc_sc.shape, F32)
    ones_rows = jnp.ones((BF16_ROWS, blk), BF16)

    def scores(n, dst_sc, dst_max_sc):
        k0 = pl.multiple_of(jnp.minimum(n, nkb - 1) * blk, blk)
        s = (_dot(k_ref[pl.ds(k0, blk), :], q4_sc[...]).astype(BF16)
             + bias_sc[jnp.clip(i - n, 0, _MOBA_BIAS_TILES - 1)])
        dst_sc[...] = s
        dst_max_sc[...] = jnp.max(s, axis=0, keepdims=True).astype(F32)

    def visit(n, src_sc, src_max_sc):
        nc = jnp.minimum(n, nkb - 1)
        chosen = sel_sc[pl.ds(nc, 1), :] > 0.0
        m_old = m_sc[...]
        m_new = jnp.maximum(m_old, jnp.where(chosen, src_max_sc[...], NEG))
        alpha = jnp.exp2(m_old - m_new)
        pb = jnp.exp2(src_sc[...] - jnp.where(chosen, m_new, _MASKED_OUT).astype(BF16))
        vt = vt_ref[nc]
        pv = jnp.concatenate(
            [_dot(jnp.concatenate([vt[h * hd:(h + 1) * hd, :], ones_rows], axis=0),
                  pb[:, h * blk:(h + 1) * blk]) for h in range(nh)], axis=1)
        acc_sc[...] = alpha * acc_sc[...] + pv[:hd]
        l_sc[...] = alpha * l_sc[...] + pv[hd:hd + 1]
        m_sc[...] = m_new

    bufs = ((s0_sc, mx0_sc), (s1_sc, mx1_sc), (s2_sc, mx2_sc), (s3_sc, mx3_sc))
    scores(0, *bufs[0])
    scores(1, *bufs[1])
    n_quads = (i + 1) // 4

    def quad_body(j, carry):
        n = 4 * j
        scores(n + 2, *bufs[2])
        visit(n, *bufs[0])
        scores(n + 3, *bufs[3])
        visit(n + 1, *bufs[1])
        scores(n + 4, *bufs[0])
        visit(n + 2, *bufs[2])
        scores(n + 5, *bufs[1])
        visit(n + 3, *bufs[3])
        return carry

    def pair_body(j, carry):
        n = 4 * n_quads + 2 * j
        visit(n, *bufs[0])
        scores(n + 2, *bufs[0])
        visit(n + 1, *bufs[1])
        scores(n + 3, *bufs[1])
        return carry

    lax.fori_loop(0, n_quads, quad_body, 0)
    lax.fori_loop(0, (i + 2 - 4 * n_quads) // 2, pair_body, 0)

    ot = acc_sc[...] / l_sc[...]
    drow = lax.broadcasted_iota(jnp.int32, (hd, BRANCH_W), 0)
    lane = lax.broadcasted_iota(jnp.int32, (hd, BRANCH_W), 1)
    out = jnp.zeros((blk, BRANCH_W), F32)
    for h in range(nh):
        place = jnp.where(lane == drow + h * hd, 1.0, 0.0).astype(BF16)
        piece = ot[:, h * blk:(h + 1) * blk]
        hi = piece.astype(BF16)
        lo = (piece - hi.astype(F32)).astype(BF16)
        out = out + _dot_tn(hi, place) + _dot_tn(lo, place)
    o_ref[...] = out


def _moba(tab, qt, k, vt):
    bsz, s, w = k.shape
    blk = MOBA_BLOCK
    nkb = s // blk
    nh = N_HEADS_BR
    return pl.pallas_call(
        _moba_kernel,
        grid=(bsz, nkb),
        in_specs=[pl.BlockSpec(memory_space=pltpu.SMEM),
                  pl.BlockSpec((None, None, w, blk), lambda b, i: (b, i, 0, 0)),
                  pl.BlockSpec((None, s, w), lambda b, i: (b, 0, 0)),
                  pl.BlockSpec((None, nkb, w, blk), lambda b, i: (b, 0, 0, 0))],
        out_specs=pl.BlockSpec((None, blk, w), lambda b, i: (b, i, 0)),
        out_shape=jax.ShapeDtypeStruct((bsz, s, w), F32),
        scratch_shapes=[pltpu.VMEM((_MOBA_BIAS_TILES, blk, nh * blk), BF16),
                        pltpu.VMEM((nkb, w), F32),
                        pltpu.VMEM((w, nh * blk), BF16),
                        pltpu.VMEM((nkb, nh * blk), F32),
                        *[pltpu.VMEM((blk, nh * blk), BF16) for _ in range(4)],
                        *[pltpu.VMEM((1, nh * blk), F32) for _ in range(4)],
                        pltpu.VMEM((HEAD_DIM, nh * blk), F32),
                        pltpu.VMEM((1, nh * blk), F32),
                        pltpu.VMEM((1, nh * blk), F32)],
        compiler_params=_params("arbitrary", "arbitrary"),
        name="moba",
    )(tab, qt, k, vt)


def _dil_kernel(tab_ref, q_ref, kc_ref, kp_ref, vc_ref, vp_ref, o_ref, lse_ref, bias_sc, *, dil, tq):
    nb = DIL_BLOCK
    nh = N_HEADS_BR
    t = pl.program_id(2)
    first = (pl.program_id(0) == 0) & (pl.program_id(1) == 0) & (t == 0)

    @pl.when(first)
    def _():
        row = lax.broadcasted_iota(jnp.int32, (nb, 2 * nb), 0)
        col = lax.broadcasted_iota(jnp.int32, (nb, 2 * nb), 1)
        dist_sub = nb + row - col
        band = (dist_sub >= 0) & (dist_sub <= nb)
        for h in range(nh):
            tile = _bias_from_dist(jnp.maximum(dist_sub, 0) * dil, tab_ref, nh + h)
            bias_sc[h * nb:(h + 1) * nb, :] = jnp.where(band, tile, NEG)

    for jb in range(tq // nb):
        lo, hi = jb * nb, (jb + 1) * nb
        q4 = _stack_heads(q_ref[lo:hi, :])
        if jb == 0:
            kprev, vprev = kp_ref[tq - nb:tq, :], vp_ref[tq - nb:tq, :]
        else:
            kprev, vprev = kc_ref[lo - nb:lo, :], vc_ref[lo - nb:lo, :]
        keys = jnp.concatenate([kprev, kc_ref[lo:hi, :]], axis=0)
        vals = jnp.concatenate([vprev, vc_ref[lo:hi, :]], axis=0)
        s = _dot_nt(q4, keys) + bias_sc[...]
        if jb == 0:
            col = lax.broadcasted_iota(jnp.int32, s.shape, 1)
            s = jnp.where((col >= nb) | (t > 0), s, NEG)
        m = jnp.max(s, axis=-1, keepdims=True)
        p = jnp.exp2(s - m)
        l = jnp.sum(p, axis=-1, keepdims=True)
        o4 = _dot(p.astype(BF16), vals) / l
        lse4 = jnp.broadcast_to((m + jnp.log2(l)) * LN2, o4.shape)
        o_ref[lo:hi, :] = _merge_heads(o4, nb)
        lse_ref[lo:hi, :] = _merge_heads(lse4, nb)


def _dilated(tab, q, k, v, dil, bsz):
    rows, width = q.shape
    w = BRANCH_W
    sub_len = rows // bsz
    tq = min(512, sub_len)
    nt = sub_len // tq
    view = lambda a: a.reshape(bsz, sub_len, width)
    cur = pl.BlockSpec((None, tq, w), lambda b, r, t: (b, t, r))
    prev = pl.BlockSpec((None, tq, w), lambda b, r, t: (b, jnp.maximum(t - 1, 0), r))
    out_shape = jax.ShapeDtypeStruct((bsz, sub_len, width), F32)
    o, lse = pl.pallas_call(
        functools.partial(_dil_kernel, dil=dil, tq=tq),
        grid=(bsz, dil, nt),
        in_specs=[pl.BlockSpec(memory_space=pltpu.SMEM), cur, cur, prev, cur, prev],
        out_specs=[cur, cur],
        out_shape=[out_shape, out_shape],
        scratch_shapes=[pltpu.VMEM((N_HEADS_BR * DIL_BLOCK, 2 * DIL_BLOCK), F32)],
        compiler_params=_params("arbitrary", "arbitrary", "arbitrary"),
        name=f"dilated{dil}",
    )(tab, view(q), view(k), view(k), view(v), view(v))
    return o.reshape(rows, width), lse.reshape(rows, width)


_S5_STEPS = (1, 2, 4)


def _s5_kernel(u_ref, wb_ref, coef_ref, wc_ref, d_ref, gw_ref, gb_ref, o_ref, h_sc, carry_sc):
    ts = u_ref.shape[0]
    nl = S5_LANES

    @pl.when(pl.program_id(1) == 0)
    def _():
        carry_sc[...] = jnp.zeros_like(carry_sc)

    u = u_ref[...]
    h_sc[...] = _dot(u.astype(BF16), wb_ref[...])

    def cfma(re, im, cr, ci, sre, sim):
        return re + cr * sre - ci * sim, im + cr * sim + ci * sre

    def body(g, carry):
        cre, cim = carry
        r0 = pl.multiple_of(g * SUBLANES, SUBLANES)
        re = h_sc[pl.ds(r0, SUBLANES), :nl]
        im = h_sc[pl.ds(r0, SUBLANES), nl:]
        for si, k in enumerate(_S5_STEPS):
            re, im = cfma(re, im, coef_ref[si, 0], coef_ref[si, 1],
                          pltpu.roll(re, k, axis=0), pltpu.roll(im, k, axis=0))
        last = len(_S5_STEPS)
        re, im = cfma(re, im, coef_ref[last, 0], coef_ref[last, 1],
                      jnp.broadcast_to(cre, re.shape), jnp.broadcast_to(cim, im.shape))
        h_sc[pl.ds(r0, SUBLANES), :nl] = re
        h_sc[pl.ds(r0, SUBLANES), nl:] = im
        return re[SUBLANES - 1:, :], im[SUBLANES - 1:, :]

    cre, cim = lax.fori_loop(0, ts // SUBLANES, body, (carry_sc[0:1, :nl], carry_sc[0:1, nl:]))
    carry_sc[0:1, :nl] = cre
    carry_sc[0:1, nl:] = cim

    y = _dot(h_sc[...].astype(BF16), wc_ref[...]) + d_ref[...] * u
    y = 0.5 * y * (1.0 + jnp.tanh(math.sqrt(2.0 / math.pi) * (y + 0.044715 * (y * y * y))))
    z = _dot(y.astype(BF16), gw_ref[...]) + gb_ref[...]
    o_ref[...] = y * jax.nn.sigmoid(z)


def _s5_weights(a_re, a_im, log_dt, b_re, b_im, c_re, c_im):
    g, p, c = b_re.shape
    ar, ai = a_re.astype(F32), a_im.astype(F32)
    dt = jnp.exp(log_dt.astype(F32))[:, None]

    def abar_pow(k):
        mag = jnp.exp(k * dt * ar)
        return (mag * jnp.cos(k * dt * ai)).reshape(-1), (mag * jnp.sin(k * dt * ai)).reshape(-1)

    abar_re, abar_im = jnp.exp(dt * ar) * jnp.cos(dt * ai), jnp.exp(dt * ar) * jnp.sin(dt * ai)
    nr, ni = abar_re - 1.0, abar_im
    den = ar * ar + ai * ai
    zr = (nr * ar + ni * ai) / den
    zi = (ni * ar - nr * ai) / den
    br, bi = b_re.astype(F32), b_im.astype(F32)
    bbar_re = zr[..., None] * br - zi[..., None] * bi
    bbar_im = zr[..., None] * bi + zi[..., None] * br
    eye = jnp.eye(g, dtype=F32)
    blockdiag_in = lambda m: jnp.einsum('gpc,gh->gchp', m, eye).reshape(g * c, g * p)
    wb = jnp.concatenate([blockdiag_in(bbar_re), blockdiag_in(bbar_im)], axis=1)
    blockdiag_out = lambda m: jnp.einsum('gcp,gh->gphc', m, eye).reshape(g * p, g * c)
    wc = jnp.concatenate([blockdiag_out(c_re.astype(F32)), -blockdiag_out(c_im.astype(F32))], axis=0)
    rows = jnp.arange(SUBLANES)[:, None]
    coefs = []
    for k in _S5_STEPS:
        pr, pi = abar_pow(k)
        coefs.append(jnp.stack([jnp.where(rows >= k, pr[None, :], 0.0),
                                jnp.where(rows >= k, pi[None, :], 0.0)]))
    per_row = [abar_pow(k + 1) for k in range(SUBLANES)]
    coefs.append(jnp.stack([jnp.stack([r for r, _ in per_row]), jnp.stack([i for _, i in per_row])]))
    return wb.astype(BF16), jnp.stack(coefs), wc.astype(BF16)


def _s5(u, wb, coef, wc, d_skip, glu_w, glu_b):
    bsz, s, w = u.shape
    ts = min(512, s)
    const = lambda shape: pl.BlockSpec(shape, lambda b, t: (0,) * len(shape))
    return pl.pallas_call(
        _s5_kernel,
        grid=(bsz, s // ts),
        in_specs=[pl.BlockSpec((None, ts, w), lambda b, t: (b, t, 0)),
                  const(wb.shape), const(coef.shape), const(wc.shape),
                  const((1, w)), const(glu_w.shape), const((1, w))],
        out_specs=pl.BlockSpec((None, ts, w), lambda b, t: (b, t, 0)),
        out_shape=jax.ShapeDtypeStruct((bsz, s, w), F32),
        scratch_shapes=[pltpu.VMEM((ts, 2 * S5_LANES), F32),
                        pltpu.VMEM((SUBLANES, 2 * S5_LANES), F32)],
        compiler_params=_params("arbitrary", "arbitrary"),
        name="s5",
    )(u, wb, coef, wc, d_skip.reshape(1, w), glu_w, glu_b.reshape(1, w))


def _hgrn_kernel(lb_ref, q_ref, f_ref, i_ref, o_ref, st_sc):
    th = q_ref.shape[0]
    c = HGRN_SUB
    w = BRANCH_W

    @pl.when(pl.program_id(1) == 0)
    def _():
        st_sc[...] = jnp.zeros_like(st_sc)

    lb = lb_ref[...]
    rowi = lax.broadcasted_iota(jnp.int32, (c, w), 0)
    same_head, head_ones = _same_head_ones()

    def body(ci, carry):
        r0 = pl.multiple_of(ci * c, c)
        q = q_ref[pl.ds(r0, c), :]
        qf = q * jax.nn.sigmoid(q) * (HEAD_DIM ** -0.5)
        f = lb + (1.0 - lb) * jax.nn.sigmoid(f_ref[pl.ds(r0, c), :])
        kk = 1.0 - f
        v = i_ref[pl.ds(r0, c), :]
        b = jnp.log(f)
        k = 1
        while k < c:
            b = b + jnp.where(rowi >= k, pltpu.roll(b, k, axis=0), 0.0)
            k *= 2
        e = jnp.concatenate(
            [jnp.exp(jnp.where(rowi >= s, b - b[s:s + 1, :], NEG)) * (qf * kk[s:s + 1, :])
             for s in range(c)], axis=0)
        wgt = _dot(e.astype(BF16), head_ones)
        o = jnp.zeros((c, w), F32)
        for s in range(c):
            o = o + wgt[s * c:(s + 1) * c, :] * v[s:s + 1, :]
        st = st_sc[...]
        o = o + _dot_nt((qf * jnp.exp(b)).astype(BF16), st.astype(BF16))
        b_last = b[c - 1:c, :]
        upd = _dot_tn(v.astype(BF16), (kk * jnp.exp(b_last - b)).astype(BF16))
        st_sc[...] = jnp.exp(b_last) * st + jnp.where(same_head, upd, 0.0)
        o_ref[pl.ds(r0, c), :] = o
        return carry

    lax.fori_loop(0, th // c, body, 0, unroll=2)


def _hgrn(lb, q, f, i):
    bsz, s, w = q.shape
    th = min(512, s)
    tile = pl.BlockSpec((None, th, w), lambda b, t: (b, t, 0))
    return pl.pallas_call(
        _hgrn_kernel,
        grid=(bsz, s // th),
        in_specs=[pl.BlockSpec((1, w), lambda b, t: (0, 0)), tile, tile, tile],
        out_specs=tile,
        out_shape=jax.ShapeDtypeStruct((bsz, s, w), F32),
        scratch_shapes=[pltpu.VMEM((w, w), F32)],
        compiler_params=_params("arbitrary", "arbitrary"),
        name="hgrn2",
    )(lb.reshape(1, w), q, f, i)


def _outproj_kernel(ya_ref, yb_ref, yc_ref, o1_ref, l1_ref, *rest, alpha):
    nres = len(_RESIDUE_DILS)
    res_refs = rest[:2 * nres]
    gates_ref, x_ref, w_ref, gain_ref, lng_ref, lnb_ref, out_ref, slab_sc = rest[2 * nres:]
    tm = x_ref.shape[0]
    halves = BRANCH_W // LANES

    def token_order(ref, dil):
        for rr in range(dil):
            for half in range(halves):
                c0 = rr * BRANCH_W + half * LANES
                slab_sc[half, pl.ds(rr, tm // dil, stride=dil), :] = ref[:, c0:c0 + LANES]
        return jnp.concatenate([slab_sc[half] for half in range(halves)], axis=1)

    outs, lses = [o1_ref[...]], [l1_ref[...]]
    for di, dil in enumerate(_RESIDUE_DILS):
        outs.append(token_order(res_refs[2 * di], dil))
        lses.append(token_order(res_refs[2 * di + 1], dil))
    lm = functools.reduce(jnp.maximum, lses)
    es = [jnp.exp(l - lm) for l in lses]
    yd = sum(e * o for e, o in zip(es, outs)) / sum(es)

    _, head_ones = _same_head_ones()
    normed = []
    for y in (ya_ref[...], yb_ref[...], yc_ref[...], yd):
        y2 = y * y
        hi = y2.astype(BF16)
        lo = (y2 - hi.astype(F32)).astype(BF16)
        ss = _dot(hi, head_ones) + _dot(lo, head_ones)
        normed.append(y * lax.rsqrt(ss * (1.0 / HEAD_DIM) + RMS_EPS))
    g = gates_ref[...]
    y = jnp.concatenate(normed, axis=-1) * gain_ref[...] * (g * jax.nn.sigmoid(g))
    r = alpha * x_ref[...] + _dot(y.astype(BF16), w_ref[...])
    mu = jnp.mean(r, axis=-1, keepdims=True)
    rc = r - mu
    var = jnp.mean(rc * rc, axis=-1, keepdims=True)
    out_ref[...] = rc * lax.rsqrt(var + LN_EPS) * lng_ref[...] + lnb_ref[...]


def _outproj(branches, residue_major, gates, xf, w, gain, ln_g, ln_b, alpha):
    t, d = xf.shape
    tm = OUT_TILE
    wide = pl.BlockSpec((tm, d), lambda i: (i, 0))
    narrow = pl.BlockSpec((tm, BRANCH_W), lambda i: (i, 0))
    vec = pl.BlockSpec((1, d), lambda i: (0, 0))
    res_specs, res_args = [], []
    for dil, (o, lse) in zip(_RESIDUE_DILS, residue_major):
        spec = pl.BlockSpec((tm // dil, dil * BRANCH_W), lambda i: (i, 0))
        res_specs += [spec, spec]
        res_args += [o, lse]
    return pl.pallas_call(
        functools.partial(_outproj_kernel, alpha=alpha),
        grid=(t // tm,),
        in_specs=[narrow] * 5 + res_specs + [wide, wide, pl.BlockSpec((d, d), lambda i: (0, 0)), vec, vec, vec],
        out_specs=wide,
        out_shape=jax.ShapeDtypeStruct((t, d), F32),
        scratch_shapes=[pltpu.VMEM((BRANCH_W // LANES, tm, LANES), F32)],
        compiler_params=_params("arbitrary"),
        name="outproj",
    )(*branches, *res_args, gates, xf, w, gain.reshape(1, d), ln_g.reshape(1, d), ln_b.reshape(1, d))


def kernel(x, w_in, rel_bias, s5_a_re, s5_a_im, s5_log_dt, s5_b_re, s5_b_im, s5_c_re, s5_c_im,
           s5_d, s5_glu_w, s5_glu_b, hgrn_lower, branch_gain, w_out, ln_g, ln_b):
    bsz, s, d = x.shape
    depth = w_in.shape[0]
    t = bsz * s
    w = BRANCH_W
    alpha = (2 * depth) ** 0.25
    p_lb = jax.nn.softmax(hgrn_lower.astype(F32), axis=0)
    lb_all = jnp.cumsum(p_lb, axis=0) - p_lb[0]
    tab = rel_bias.astype(F32)
    w_in_b = w_in.astype(BF16)
    w_out_b = w_out.astype(BF16)
    glu_w_b = s5_glu_w.astype(BF16)

    xf = x.reshape(t, d).astype(F32)
    for l in range(depth):
        wvt = jnp.concatenate([w_in_b[l][:, ts * w:(ts + 1) * w].T for ts in _MOBA_T_SLOTS], axis=0)
        plain, by_dil = _inproj(xf, w_in_b[l], wvt)
        qat, ka, vat, us, qc, fc, ic, qd, kd, vd, gates = plain
        seq = lambda a: a.reshape(bsz, s, w)
        per_block = lambda a: a.reshape(bsz, s // MOBA_BLOCK, w, MOBA_BLOCK)
        ya = _moba(tab, per_block(qat), seq(ka), per_block(vat))
        wb, coef, wc = _s5_weights(s5_a_re[l], s5_a_im[l], s5_log_dt[l], s5_b_re[l], s5_b_im[l],
                                   s5_c_re[l], s5_c_im[l])
        yb = _s5(seq(us), wb, coef, wc, s5_d[l].astype(F32), glu_w_b[l], s5_glu_b[l].astype(F32))
        yc = _hgrn(lb_all[l], seq(qc), seq(fc), seq(ic))
        o1, lse1 = _dilated(tab, qd, kd, vd, 1, bsz)
        residue_major = [_dilated(tab, *by_dil[dil], dil, bsz) for dil in _RESIDUE_DILS]
        flat = lambda a: a.reshape(t, w)
        xf = _outproj([flat(ya), flat(yb), flat(yc), o1, lse1], residue_major, gates, xf, w_out_b[l],
                      branch_gain[l].astype(F32), ln_g[l].astype(F32), ln_b[l].astype(F32), alpha)
    return xf.reshape(bsz, s, d).astype(x.dtype)
```

```python
import functools
import math

import numpy as np
import jax
import jax.numpy as jnp
from jax import lax
from jax.experimental import pallas as pl
from jax.experimental.pallas import tpu as pltpu

F32 = jnp.float32
BF16 = jnp.bfloat16

HEAD_DIM = 64
N_HEADS_BR = 4
BRANCH_W = N_HEADS_BR * HEAD_DIM
N_IN_SLOTS = 10
MOBA_BLOCK = 256
MOBA_TOPK = 3
S5_GROUP = 16
S5_GROUPS = BRANCH_W // S5_GROUP
S5_STATE = 64
S5_LANES = S5_GROUPS * S5_STATE
DIL_PAIRS = ((128, 1), (512, 4), (2048, 16))
DIL_BLOCK = 128
REL_BUCKETS = 32
REL_MAX_DIST = 2048
LN_EPS = 1e-5
RMS_EPS = 1e-6
NEG = -1e30
HGRN_SUB = 16
SUBLANES = 8
LANES = 128
BF16_ROWS = 16
V7X_VMEM_LIMIT = 56 * 1024 * 1024
LOG2E = math.log2(math.e)
LN2 = math.log(2.0)
QK_SCALE = HEAD_DIM ** -0.5 * LOG2E
IN_TILE = 512
OUT_TILE = 256


def _bucket_thresholds():
    max_exact = REL_BUCKETS // 2
    d = np.arange(0, 4 * REL_MAX_DIST, dtype=np.int64)
    large = max_exact + (np.log(np.maximum(d, 1) / max_exact) / math.log(REL_MAX_DIST / max_exact)
                         * (REL_BUCKETS - max_exact)).astype(np.int64)
    bucket = np.where(d < max_exact, d, np.minimum(large, REL_BUCKETS - 1))
    assert np.all(np.diff(bucket) >= 0)
    return [int(np.argmax(bucket >= j)) for j in range(REL_BUCKETS)]


_BUCKET_THR = _bucket_thresholds()


def _bias_from_dist(dist, tab_ref, head):
    val = jnp.full(dist.shape, tab_ref[0, head] * LOG2E, F32)
    for j in range(1, REL_BUCKETS):
        val = jnp.where(dist >= _BUCKET_THR[j], tab_ref[j, head] * LOG2E, val)
    return val


def _dot_nt(a, b):
    return lax.dot_general(a, b, (((1,), (1,)), ((), ())), preferred_element_type=F32)


def _dot_tn(a, b):
    return lax.dot_general(a, b, (((0,), (0,)), ((), ())), preferred_element_type=F32)


def _dot(a, b):
    return jnp.dot(a, b, preferred_element_type=F32)


def _lane_head(shape):
    return lax.shift_right_logical(lax.broadcasted_iota(jnp.int32, shape, 1), 6)


def _stack_heads(q):
    lh = _lane_head(q.shape)
    zero = jnp.zeros_like(q)
    return jnp.concatenate([jnp.where(lh == h, q, zero) for h in range(N_HEADS_BR)], axis=0)


def _merge_heads(o4, n):
    lh = _lane_head((n, BRANCH_W))
    out = jnp.zeros((n, BRANCH_W), F32)
    for h in range(N_HEADS_BR):
        out = jnp.where(lh == h, o4[h * n:(h + 1) * n, :], out)
    return out


def _same_head_ones():
    w = BRANCH_W
    same = (lax.shift_right_logical(lax.broadcasted_iota(jnp.int32, (w, w), 0), 6)
            == lax.shift_right_logical(lax.broadcasted_iota(jnp.int32, (w, w), 1), 6))
    return same, jnp.where(same, 1.0, 0.0).astype(BF16)


def _params(*sem):
    return pltpu.CompilerParams(dimension_semantics=sem, vmem_limit_bytes=V7X_VMEM_LIMIT)


_BF16_SLOTS = (0, 1, 2, 7, 8, 9)
_Q_SLOTS = (0, 7)
_MOBA_T_SLOTS = (0, 2)
_DIL_SLOTS = (7, 8, 9)
_RESIDUE_DILS = tuple(dil for _, dil in DIL_PAIRS if dil > 1)


def _inproj_kernel(x_ref, w_ref, wvt_ref, *refs):
    n_plain = N_IN_SLOTS + 1
    out_refs = refs[:n_plain]
    res_refs = refs[n_plain:n_plain + len(_DIL_SLOTS) * len(_RESIDUE_DILS)]
    slab_sc = refs[-1]
    xb = x_ref[...].astype(BF16)
    tm = xb.shape[0]
    blk = MOBA_BLOCK
    for s in range(N_IN_SLOTS):
        if s in _MOBA_T_SLOTS:
            ti = _MOBA_T_SLOTS.index(s)
            rt = _dot_nt(wvt_ref[ti * BRANCH_W:(ti + 1) * BRANCH_W, :], xb)
            if s in _Q_SLOTS:
                rt = rt * QK_SCALE
            rt = rt.astype(BF16)
            for j in range(tm // blk):
                out_refs[s][j] = rt[:, j * blk:(j + 1) * blk]
            continue
        r = _dot(xb, w_ref[:, s * BRANCH_W:(s + 1) * BRANCH_W])
        if s in _Q_SLOTS:
            r = r * QK_SCALE
        out_refs[s][...] = r.astype(out_refs[s].dtype)
        if s in _DIL_SLOTS:
            for half in range(BRANCH_W // LANES):
                slab_sc[half] = r[:, half * LANES:(half + 1) * LANES]
            for di, dil in enumerate(_RESIDUE_DILS):
                o_ref = res_refs[_DIL_SLOTS.index(s) * len(_RESIDUE_DILS) + di]
                for rr in range(dil):
                    for half in range(BRANCH_W // LANES):
                        c0 = rr * BRANCH_W + half * LANES
                        o_ref[:, c0:c0 + LANES] = slab_sc[half, pl.ds(rr, tm // dil, stride=dil), :].astype(BF16)
    g0 = N_IN_SLOTS * BRANCH_W
    out_refs[N_IN_SLOTS][...] = _dot(xb, w_ref[:, g0:])


def _inproj(xf, w, wvt):
    t, d = xf.shape
    tm = IN_TILE
    ncol = w.shape[1]
    blk = MOBA_BLOCK
    shapes = [jax.ShapeDtypeStruct((t, BRANCH_W), BF16 if s in _BF16_SLOTS else F32)
              for s in range(N_IN_SLOTS)]
    shapes.append(jax.ShapeDtypeStruct((t, ncol - N_IN_SLOTS * BRANCH_W), F32))
    out_specs = [pl.BlockSpec((tm, sh.shape[1]), lambda i: (i, 0)) for sh in shapes]
    for s in _MOBA_T_SLOTS:
        shapes[s] = jax.ShapeDtypeStruct((t // blk, BRANCH_W, blk), BF16)
        out_specs[s] = pl.BlockSpec((tm // blk, BRANCH_W, blk), lambda i: (i, 0, 0))
    for _ in _DIL_SLOTS:
        for dil in _RESIDUE_DILS:
            shapes.append(jax.ShapeDtypeStruct((t // dil, dil * BRANCH_W), BF16))
            out_specs.append(pl.BlockSpec((tm // dil, dil * BRANCH_W), lambda i: (i, 0)))
    outs = pl.pallas_call(
        _inproj_kernel,
        grid=(t // tm,),
        in_specs=[pl.BlockSpec((tm, d), lambda i: (i, 0)),
                  pl.BlockSpec((d, ncol), lambda i: (0, 0)),
                  pl.BlockSpec((len(_MOBA_T_SLOTS) * BRANCH_W, d), lambda i: (0, 0))],
        out_specs=out_specs,
        out_shape=shapes,
        scratch_shapes=[pltpu.VMEM((BRANCH_W // LANES, tm, LANES), F32)],
        compiler_params=_params("arbitrary"),
        name="inproj",
    )(xf, w, wvt)
    plain, res = outs[:N_IN_SLOTS + 1], outs[N_IN_SLOTS + 1:]
    nres = len(_RESIDUE_DILS)
    by_dil = {dil: [res[si * nres + di] for si in range(len(_DIL_SLOTS))]
              for di, dil in enumerate(_RESIDUE_DILS)}
    return plain, by_dil


_MOBA_BIAS_TILES = REL_MAX_DIST // MOBA_BLOCK + 2
_MASKED_OUT = 1e30


def _moba_kernel(tab_ref, q_ref, k_ref, vt_ref, o_ref,
                 bias_sc, kmean_sc, q4_sc, sel_sc, s0_sc, s1_sc, s2_sc, s3_sc,
                 mx0_sc, mx1_sc, mx2_sc, mx3_sc, acc_sc, m_sc, l_sc):
    b = pl.program_id(0)
    i = pl.program_id(1)
    blk = MOBA_BLOCK
    nkb = k_ref.shape[0] // blk
    nh = N_HEADS_BR
    hd = HEAD_DIM

    @pl.when((b == 0) & (i == 0))
    def _():
        key = lax.broadcasted_iota(jnp.int32, (blk, blk), 0)
        qry = lax.broadcasted_iota(jnp.int32, (blk, blk), 1)
        for m in range(_MOBA_BIAS_TILES):
            for h in range(nh):
                if m == _MOBA_BIAS_TILES - 1:
                    tile = jnp.full((blk, blk), tab_ref[REL_BUCKETS - 1, h] * LOG2E, F32)
                else:
                    tile = _bias_from_dist(jnp.maximum(m * blk + qry - key, 0), tab_ref, h)
                if m == 0:
                    tile = jnp.where(qry >= key, tile, NEG)
                bias_sc[m, :, h * blk:(h + 1) * blk] = tile.astype(BF16)

    @pl.when(i == 0)
    def _():
        for n in range(nkb):
            kb = k_ref[n * blk:(n + 1) * blk, :].astype(F32)
            kmean_sc[n:n + 1, :] = jnp.sum(kb, axis=0, keepdims=True) * (1.0 / blk)

    qt = q_ref[...]
    row_head = lax.shift_right_logical(lax.broadcasted_iota(jnp.int32, qt.shape, 0), 6)
    q4_sc[...] = jnp.concatenate(
        [jnp.where(row_head == h, qt, jnp.zeros_like(qt)) for h in range(nh)], axis=1)
    q4 = q4_sc[...]

    km = kmean_sc[...]
    km_hi = km.astype(BF16)
    km_lo = (km - km_hi.astype(F32)).astype(BF16)
    gate = _dot(km_hi, q4) + _dot(km_lo, q4)
    blk_id = lax.broadcasted_iota(jnp.int32, gate.shape, 0)
    gate = jnp.where(blk_id < i, gate, -jnp.inf)
    sel = jnp.where(blk_id == i, 1.0, 0.0)
    for _ in range(MOBA_TOPK):
        mx = jnp.max(gate, axis=0, keepdims=True)
        idx = jnp.min(jnp.where(gate == mx, blk_id, nkb), axis=0, keepdims=True)
        pick = (blk_id == idx) & (mx > -jnp.inf)
        sel = jnp.where(pick, 1.0, sel)
        gate = jnp.where(pick, -jnp.inf, gate)
    sel_sc[...] = sel

    m_sc[...] = jnp.full(m_sc.shape, NEG, F32)
    l_sc[...] = jnp.zeros(l_sc.shape, F32)
    acc_sc[...] = jnp.zeros(acc_sc.shape, F32)
    ones_rows = jnp.ones((BF16_ROWS, blk), BF16)

    def scores(n, dst_sc, dst_max_sc):
        k0 = pl.multiple_of(jnp.minimum(n, nkb - 1) * blk, blk)
        s = (_dot(k_ref[pl.ds(k0, blk), :], q4_sc[...]).astype(BF16)
             + bias_sc[jnp.clip(i - n, 0, _MOBA_BIAS_TILES - 1)])
        dst_sc[...] = s
        dst_max_sc[...] = jnp.max(s, axis=0, keepdims=True).astype(F32)

    def visit(n, src_sc, src_max_sc):
        nc = jnp.minimum(n, nkb - 1)
        chosen = sel_sc[pl.ds(nc, 1), :] > 0.0
        m_old = m_sc[...]
        m_new = jnp.maximum(m_old, jnp.where(chosen, src_max_sc[...], NEG))
        alpha = jnp.exp2(m_old - m_new)
        pb = jnp.exp2(src_sc[...] - jnp.where(chosen, m_new, _MASKED_OUT).astype(BF16))
        vt = vt_ref[nc]
        pv = jnp.concatenate(
            [_dot(jnp.concatenate([vt[h * hd:(h + 1) * hd, :], ones_rows], axis=0),
                  pb[:, h * blk:(h + 1) * blk]) for h in range(nh)], axis=1)
        acc_sc[...] = alpha * acc_sc[...] + pv[:hd]
        l_sc[...] = alpha * l_sc[...] + pv[hd:hd + 1]
        m_sc[...] = m_new

    bufs = ((s0_sc, mx0_sc), (s1_sc, mx1_sc), (s2_sc, mx2_sc), (s3_sc, mx3_sc))
    scores(0, *bufs[0])
    scores(1, *bufs[1])
    n_quads = (i + 1) // 4

    def quad_body(j, carry):
        n = 4 * j
        scores(n + 2, *bufs[2])
        visit(n, *bufs[0])
        scores(n + 3, *bufs[3])
        visit(n + 1, *bufs[1])
        scores(n + 4, *bufs[0])
        visit(n + 2, *bufs[2])
        scores(n + 5, *bufs[1])
        visit(n + 3, *bufs[3])
        return carry

    def pair_body(j, carry):
        n = 4 * n_quads + 2 * j
        visit(n, *bufs[0])
        scores(n + 2, *bufs[0])
        visit(n + 1, *bufs[1])
        scores(n + 3, *bufs[1])
        return carry

    lax.fori_loop(0, n_quads, quad_body, 0)
    lax.fori_loop(0, (i + 2 - 4 * n_quads) // 2, pair_body, 0)

    ot = acc_sc[...] / l_sc[...]
    drow = lax.broadcasted_iota(jnp.int32, (hd, BRANCH_W), 0)
    lane = lax.broadcasted_iota(jnp.int32, (hd, BRANCH_W), 1)
    out = jnp.zeros((blk, BRANCH_W), F32)
    for h in range(nh):
        place = jnp.where(lane == drow + h * hd, 1.0, 0.0).astype(BF16)
        piece = ot[:, h * blk:(h + 1) * blk]
        hi = piece.astype(BF16)
        lo = (piece - hi.astype(F32)).astype(BF16)
        out = out + _dot_tn(hi, place) + _dot_tn(lo, place)
    o_ref[...] = out


def _moba(tab, qt, k, vt):
    bsz, s, w = k.shape
    blk = MOBA_BLOCK
    nkb = s // blk
    nh = N_HEADS_BR
    return pl.pallas_call(
        _moba_kernel,
        grid=(bsz, nkb),
        in_specs=[pl.BlockSpec(memory_space=pltpu.SMEM),
                  pl.BlockSpec((None, None, w, blk), lambda b, i: (b, i, 0, 0)),
                  pl.BlockSpec((None, s, w), lambda b, i: (b, 0, 0)),
                  pl.BlockSpec((None, nkb, w, blk), lambda b, i: (b, 0, 0, 0))],
        out_specs=pl.BlockSpec((None, blk, w), lambda b, i: (b, i, 0)),
        out_shape=jax.ShapeDtypeStruct((bsz, s, w), F32),
        scratch_shapes=[pltpu.VMEM((_MOBA_BIAS_TILES, blk, nh * blk), BF16),
                        pltpu.VMEM((nkb, w), F32),
                        pltpu.VMEM((w, nh * blk), BF16),
                        pltpu.VMEM((nkb, nh * blk), F32),
                        *[pltpu.VMEM((blk, nh * blk), BF16) for _ in range(4)],
                        *[pltpu.VMEM((1, nh * blk), F32) for _ in range(4)],
                        pltpu.VMEM((HEAD_DIM, nh * blk), F32),
                        pltpu.VMEM((1, nh * blk), F32),
                        pltpu.VMEM((1, nh * blk), F32)],
        compiler_params=_params("arbitrary", "arbitrary"),
        name="moba",
    )(tab, qt, k, vt)


def _dil_kernel(tab_ref, q_ref, kc_ref, kp_ref, vc_ref, vp_ref, o_ref, lse_ref, bias_sc, *, dil, tq):
    nb = DIL_BLOCK
    nh = N_HEADS_BR
    t = pl.program_id(2)
    first = (pl.program_id(0) == 0) & (pl.program_id(1) == 0) & (t == 0)

    @pl.when(first)
    def _():
        row = lax.broadcasted_iota(jnp.int32, (nb, 2 * nb), 0)
        col = lax.broadcasted_iota(jnp.int32, (nb, 2 * nb), 1)
        dist_sub = nb + row - col
        band = (dist_sub >= 0) & (dist_sub <= nb)
        for h in range(nh):
            tile = _bias_from_dist(jnp.maximum(dist_sub, 0) * dil, tab_ref, nh + h)
            bias_sc[h * nb:(h + 1) * nb, :] = jnp.where(band, tile, NEG)

    for jb in range(tq // nb):
        lo, hi = jb * nb, (jb + 1) * nb
        q4 = _stack_heads(q_ref[lo:hi, :])
        if jb == 0:
            kprev, vprev = kp_ref[tq - nb:tq, :], vp_ref[tq - nb:tq, :]
        else:
            kprev, vprev = kc_ref[lo - nb:lo, :], vc_ref[lo - nb:lo, :]
        keys = jnp.concatenate([kprev, kc_ref[lo:hi, :]], axis=0)
        vals = jnp.concatenate([vprev, vc_ref[lo:hi, :]], axis=0)
        s = _dot_nt(q4, keys) + bias_sc[...]
        if jb == 0:
            col = lax.broadcasted_iota(jnp.int32, s.shape, 1)
            s = jnp.where((col >= nb) | (t > 0), s, NEG)
        m = jnp.max(s, axis=-1, keepdims=True)
        p = jnp.exp2(s - m)
        l = jnp.sum(p, axis=-1, keepdims=True)
        o4 = _dot(p.astype(BF16), vals) / l
        lse4 = jnp.broadcast_to((m + jnp.log2(l)) * LN2, o4.shape)
        o_ref[lo:hi, :] = _merge_heads(o4, nb)
        lse_ref[lo:hi, :] = _merge_heads(lse4, nb)


def _dilated(tab, q, k, v, dil, bsz):
    rows, width = q.shape
    w = BRANCH_W
    sub_len = rows // bsz
    tq = min(512, sub_len)
    nt = sub_len // tq
    view = lambda a: a.reshape(bsz, sub_len, width)
    cur = pl.BlockSpec((None, tq, w), lambda b, r, t: (b, t, r))
    prev = pl.BlockSpec((None, tq, w), lambda b, r, t: (b, jnp.maximum(t - 1, 0), r))
    out_shape = jax.ShapeDtypeStruct((bsz, sub_len, width), F32)
    o, lse = pl.pallas_call(
        functools.partial(_dil_kernel, dil=dil, tq=tq),
        grid=(bsz, dil, nt),
        in_specs=[pl.BlockSpec(memory_space=pltpu.SMEM), cur, cur, prev, cur, prev],
        out_specs=[cur, cur],
        out_shape=[out_shape, out_shape],
        scratch_shapes=[pltpu.VMEM((N_HEADS_BR * DIL_BLOCK, 2 * DIL_BLOCK), F32)],
        compiler_params=_params("arbitrary", "arbitrary", "arbitrary"),
        name=f"dilated{dil}",
    )(tab, view(q), view(k), view(k), view(v), view(v))
    return o.reshape(rows, width), lse.reshape(rows, width)


_S5_STEPS = (1, 2, 4)


def _s5_kernel(u_ref, wb_ref, coef_ref, wc_ref, d_ref, gw_ref, gb_ref, o_ref, h_sc, carry_sc):
    ts = u_ref.shape[0]
    nl = S5_LANES

    @pl.when(pl.program_id(1) == 0)
    def _():
        carry_sc[...] = jnp.zeros_like(carry_sc)

    u = u_ref[...]
    h_sc[...] = _dot(u.astype(BF16), wb_ref[...])

    def cfma(re, im, cr, ci, sre, sim):
        return re + cr * sre - ci * sim, im + cr * sim + ci * sre

    def body(g, carry):
        cre, cim = carry
        r0 = pl.multiple_of(g * SUBLANES, SUBLANES)
        re = h_sc[pl.ds(r0, SUBLANES), :nl]
        im = h_sc[pl.ds(r0, SUBLANES), nl:]
        for si, k in enumerate(_S5_STEPS):
            re, im = cfma(re, im, coef_ref[si, 0], coef_ref[si, 1],
                          pltpu.roll(re, k, axis=0), pltpu.roll(im, k, axis=0))
        last = len(_S5_STEPS)
        re, im = cfma(re, im, coef_ref[last, 0], coef_ref[last, 1],
                      jnp.broadcast_to(cre, re.shape), jnp.broadcast_to(cim, im.shape))
        h_sc[pl.ds(r0, SUBLANES), :nl] = re
        h_sc[pl.ds(r0, SUBLANES), nl:] = im
        return re[SUBLANES - 1:, :], im[SUBLANES - 1:, :]

    cre, cim = lax.fori_loop(0, ts // SUBLANES, body, (carry_sc[0:1, :nl], carry_sc[0:1, nl:]))
    carry_sc[0:1, :nl] = cre
    carry_sc[0:1, nl:] = cim

    y = _dot(h_sc[...].astype(BF16), wc_ref[...]) + d_ref[...] * u
    y = 0.5 * y * (1.0 + jnp.tanh(math.sqrt(2.0 / math.pi) * (y + 0.044715 * (y * y * y))))
    z = _dot(y.astype(BF16), gw_ref[...]) + gb_ref[...]
    o_ref[...] = y * jax.nn.sigmoid(z)


def _s5_weights(a_re, a_im, log_dt, b_re, b_im, c_re, c_im):
    g, p, c = b_re.shape
    ar, ai = a_re.astype(F32), a_im.astype(F32)
    dt = jnp.exp(log_dt.astype(F32))[:, None]

    def abar_pow(k):
        mag = jnp.exp(k * dt * ar)
        return (mag * jnp.cos(k * dt * ai)).reshape(-1), (mag * jnp.sin(k * dt * ai)).reshape(-1)

    abar_re, abar_im = jnp.exp(dt * ar) * jnp.cos(dt * ai), jnp.exp(dt * ar) * jnp.sin(dt * ai)
    nr, ni = abar_re - 1.0, abar_im
    den = ar * ar + ai * ai
    zr = (nr * ar + ni * ai) / den
    zi = (ni * ar - nr * ai) / den
    br, bi = b_re.astype(F32), b_im.astype(F32)
    bbar_re = zr[..., None] * br - zi[..., None] * bi
    bbar_im = zr[..., None] * bi + zi[..., None] * br
    eye = jnp.eye(g, dtype=F32)
    blockdiag_in = lambda m: jnp.einsum('gpc,gh->gchp', m, eye).reshape(g * c, g * p)
    wb = jnp.concatenate([blockdiag_in(bbar_re), blockdiag_in(bbar_im)], axis=1)
    blockdiag_out = lambda m: jnp.einsum('gcp,gh->gphc', m, eye).reshape(g * p, g * c)
    wc = jnp.concatenate([blockdiag_out(c_re.astype(F32)), -blockdiag_out(c_im.astype(F32))], axis=0)
    rows = jnp.arange(SUBLANES)[:, None]
    coefs = []
    for k in _S5_STEPS:
        pr, pi = abar_pow(k)
        coefs.append(jnp.stack([jnp.where(rows >= k, pr[None, :], 0.0),
                                jnp.where(rows >= k, pi[None, :], 0.0)]))
    per_row = [abar_pow(k + 1) for k in range(SUBLANES)]
    coefs.append(jnp.stack([jnp.stack([r for r, _ in per_row]), jnp.stack([i for _, i in per_row])]))
    return wb.astype(BF16), jnp.stack(coefs), wc.astype(BF16)


def _s5(u, wb, coef, wc, d_skip, glu_w, glu_b):
    bsz, s, w = u.shape
    ts = min(512, s)
    const = lambda shape: pl.BlockSpec(shape, lambda b, t: (0,) * len(shape))
    return pl.pallas_call(
        _s5_kernel,
        grid=(bsz, s // ts),
        in_specs=[pl.BlockSpec((None, ts, w), lambda b, t: (b, t, 0)),
                  const(wb.shape), const(coef.shape), const(wc.shape),
                  const((1, w)), const(glu_w.shape), const((1, w))],
        out_specs=pl.BlockSpec((None, ts, w), lambda b, t: (b, t, 0)),
        out_shape=jax.ShapeDtypeStruct((bsz, s, w), F32),
        scratch_shapes=[pltpu.VMEM((ts, 2 * S5_LANES), F32),
                        pltpu.VMEM((SUBLANES, 2 * S5_LANES), F32)],
        compiler_params=_params("arbitrary", "arbitrary"),
        name="s5",
    )(u, wb, coef, wc, d_skip.reshape(1, w), glu_w, glu_b.reshape(1, w))


HGRN_SEQS = 2


def _hgrn_kernel(lb_ref, q_ref, f_ref, i_ref, o_ref, st_sc):
    nseq, th, w = q_ref.shape
    c = HGRN_SUB

    @pl.when(pl.program_id(1) == 0)
    def _():
        st_sc[...] = jnp.zeros_like(st_sc)

    lb = lb_ref[...]
    rowi = lax.broadcasted_iota(jnp.int32, (c, w), 0)
    same_head, head_ones = _same_head_ones()

    def step(bi, r0):
        q = q_ref[bi, pl.ds(r0, c), :]
        qf = q * jax.nn.sigmoid(q) * (HEAD_DIM ** -0.5)
        f = lb + (1.0 - lb) * jax.nn.sigmoid(f_ref[bi, pl.ds(r0, c), :])
        kk = 1.0 - f
        v = i_ref[bi, pl.ds(r0, c), :]
        b = jnp.log(f)
        k = 1
        while k < c:
            b = b + jnp.where(rowi >= k, pltpu.roll(b, k, axis=0), 0.0)
            k *= 2
        e = jnp.concatenate(
            [jnp.exp(jnp.where(rowi >= s, b - b[s:s + 1, :], NEG)) * (qf * kk[s:s + 1, :])
             for s in range(c)], axis=0)
        wgt = _dot(e.astype(BF16), head_ones)
        o = jnp.zeros((c, w), F32)
        for s in range(c):
            o = o + wgt[s * c:(s + 1) * c, :] * v[s:s + 1, :]
        st = st_sc[bi]
        o = o + _dot_nt((qf * jnp.exp(b)).astype(BF16), st.astype(BF16))
        b_last = b[c - 1:c, :]
        upd = _dot_tn(v.astype(BF16), (kk * jnp.exp(b_last - b)).astype(BF16))
        st_sc[bi] = jnp.exp(b_last) * st + jnp.where(same_head, upd, 0.0)
        o_ref[bi, pl.ds(r0, c), :] = o

    def body(ci, carry):
        r0 = pl.multiple_of(ci * c, c)
        for bi in range(nseq):
            step(bi, r0)
        return carry

    lax.fori_loop(0, th // c, body, 0)


def _hgrn(lb, q, f, i):
    bsz, s, w = q.shape
    th = min(512, s)
    nseq = HGRN_SEQS if bsz % HGRN_SEQS == 0 else 1
    tile = pl.BlockSpec((nseq, th, w), lambda b, t: (b, t, 0))
    return pl.pallas_call(
        _hgrn_kernel,
        grid=(bsz // nseq, s // th),
        in_specs=[pl.BlockSpec((1, w), lambda b, t: (0, 0)), tile, tile, tile],
        out_specs=tile,
        out_shape=jax.ShapeDtypeStruct((bsz, s, w), F32),
        scratch_shapes=[pltpu.VMEM((nseq, w, w), F32)],
        compiler_params=_params("arbitrary", "arbitrary"),
        name="hgrn2",
    )(lb.reshape(1, w), q, f, i)


def _outproj_kernel(ya_ref, yb_ref, yc_ref, o1_ref, l1_ref, *rest, alpha):
    nres = len(_RESIDUE_DILS)
    res_refs = rest[:2 * nres]
    gates_ref, x_ref, w_ref, gain_ref, lng_ref, lnb_ref, out_ref, slab_sc = rest[2 * nres:]
    tm = x_ref.shape[0]
    halves = BRANCH_W // LANES

    def token_order(ref, dil):
        for rr in range(dil):
            for half in range(halves):
                c0 = rr * BRANCH_W + half * LANES
                slab_sc[half, pl.ds(rr, tm // dil, stride=dil), :] = ref[:, c0:c0 + LANES]
        return jnp.concatenate([slab_sc[half] for half in range(halves)], axis=1)

    outs, lses = [o1_ref[...]], [l1_ref[...]]
    for di, dil in enumerate(_RESIDUE_DILS):
        outs.append(token_order(res_refs[2 * di], dil))
        lses.append(token_order(res_refs[2 * di + 1], dil))
    lm = functools.reduce(jnp.maximum, lses)
    es = [jnp.exp(l - lm) for l in lses]
    yd = sum(e * o for e, o in zip(es, outs)) / sum(es)

    _, head_ones = _same_head_ones()
    normed = []
    for y in (ya_ref[...], yb_ref[...], yc_ref[...], yd):
        y2 = y * y
        hi = y2.astype(BF16)
        lo = (y2 - hi.astype(F32)).astype(BF16)
        ss = _dot(hi, head_ones) + _dot(lo, head_ones)
        normed.append(y * lax.rsqrt(ss * (1.0 / HEAD_DIM) + RMS_EPS))
    g = gates_ref[...]
    y = jnp.concatenate(normed, axis=-1) * gain_ref[...] * (g * jax.nn.sigmoid(g))
    r = alpha * x_ref[...] + _dot(y.astype(BF16), w_ref[...])
    mu = jnp.mean(r, axis=-1, keepdims=True)
    rc = r - mu
    var = jnp.mean(rc * rc, axis=-1, keepdims=True)
    out_ref[...] = rc * lax.rsqrt(var + LN_EPS) * lng_ref[...] + lnb_ref[...]


def _outproj(branches, residue_major, gates, xf, w, gain, ln_g, ln_b, alpha):
    t, d = xf.shape
    tm = OUT_TILE
    wide = pl.BlockSpec((tm, d), lambda i: (i, 0))
    narrow = pl.BlockSpec((tm, BRANCH_W), lambda i: (i, 0))
    vec = pl.BlockSpec((1, d), lambda i: (0, 0))
    res_specs, res_args = [], []
    for dil, (o, lse) in zip(_RESIDUE_DILS, residue_major):
        spec = pl.BlockSpec((tm // dil, dil * BRANCH_W), lambda i: (i, 0))
        res_specs += [spec, spec]
        res_args += [o, lse]
    return pl.pallas_call(
        functools.partial(_outproj_kernel, alpha=alpha),
        grid=(t // tm,),
        in_specs=[narrow] * 5 + res_specs + [wide, wide, pl.BlockSpec((d, d), lambda i: (0, 0)), vec, vec, vec],
        out_specs=wide,
        out_shape=jax.ShapeDtypeStruct((t, d), F32),
        scratch_shapes=[pltpu.VMEM((BRANCH_W // LANES, tm, LANES), F32)],
        compiler_params=_params("arbitrary"),
        name="outproj",
    )(*branches, *res_args, gates, xf, w, gain.reshape(1, d), ln_g.reshape(1, d), ln_b.reshape(1, d))


def kernel(x, w_in, rel_bias, s5_a_re, s5_a_im, s5_log_dt, s5_b_re, s5_b_im, s5_c_re, s5_c_im,
           s5_d, s5_glu_w, s5_glu_b, hgrn_lower, branch_gain, w_out, ln_g, ln_b):
    bsz, s, d = x.shape
    depth = w_in.shape[0]
    t = bsz * s
    w = BRANCH_W
    alpha = (2 * depth) ** 0.25
    p_lb = jax.nn.softmax(hgrn_lower.astype(F32), axis=0)
    lb_all = jnp.cumsum(p_lb, axis=0) - p_lb[0]
    tab = rel_bias.astype(F32)
    w_in_b = w_in.astype(BF16)
    w_out_b = w_out.astype(BF16)
    glu_w_b = s5_glu_w.astype(BF16)

    xf = x.reshape(t, d).astype(F32)
    for l in range(depth):
        wvt = jnp.concatenate([w_in_b[l][:, ts * w:(ts + 1) * w].T for ts in _MOBA_T_SLOTS], axis=0)
        plain, by_dil = _inproj(xf, w_in_b[l], wvt)
        qat, ka, vat, us, qc, fc, ic, qd, kd, vd, gates = plain
        seq = lambda a: a.reshape(bsz, s, w)
        per_block = lambda a: a.reshape(bsz, s // MOBA_BLOCK, w, MOBA_BLOCK)
        ya = _moba(tab, per_block(qat), seq(ka), per_block(vat))
        wb, coef, wc = _s5_weights(s5_a_re[l], s5_a_im[l], s5_log_dt[l], s5_b_re[l], s5_b_im[l],
                                   s5_c_re[l], s5_c_im[l])
        yb = _s5(seq(us), wb, coef, wc, s5_d[l].astype(F32), glu_w_b[l], s5_glu_b[l].astype(F32))
        yc = _hgrn(lb_all[l], seq(qc), seq(fc), seq(ic))
        o1, lse1 = _dilated(tab, qd, kd, vd, 1, bsz)
        residue_major = [_dilated(tab, *by_dil[dil], dil, bsz) for dil in _RESIDUE_DILS]
        flat = lambda a: a.reshape(t, w)
        xf = _outproj([flat(ya), flat(yb), flat(yc), o1, lse1], residue_major, gates, xf, w_out_b[l],
                      branch_gain[l].astype(F32), ln_g[l].astype(F32), ln_b[l].astype(F32), alpha)
    return xf.reshape(bsz, s, d).astype(x.dtype)
```

```python
import functools
import math

import numpy as np
import jax
import jax.numpy as jnp
from jax import lax
from jax.experimental import pallas as pl
from jax.experimental.pallas import tpu as pltpu

F32 = jnp.float32
BF16 = jnp.bfloat16

HEAD_DIM = 64
N_HEADS_BR = 4
BRANCH_W = N_HEADS_BR * HEAD_DIM
N_IN_SLOTS = 10
MOBA_BLOCK = 256
MOBA_TOPK = 3
S5_GROUP = 16
S5_GROUPS = BRANCH_W // S5_GROUP
S5_STATE = 64
S5_LANES = S5_GROUPS * S5_STATE
DIL_PAIRS = ((128, 1), (512, 4), (2048, 16))
DIL_BLOCK = 128
REL_BUCKETS = 32
REL_MAX_DIST = 2048
LN_EPS = 1e-5
RMS_EPS = 1e-6
NEG = -1e30
HGRN_SUB = 16
SUBLANES = 8
LANES = 128
BF16_ROWS = 16
V7X_VMEM_LIMIT = 56 * 1024 * 1024
LOG2E = math.log2(math.e)
LN2 = math.log(2.0)
QK_SCALE = HEAD_DIM ** -0.5 * LOG2E
IN_TILE = 512
OUT_TILE = 256


def _bucket_thresholds():
    max_exact = REL_BUCKETS // 2
    d = np.arange(0, 4 * REL_MAX_DIST, dtype=np.int64)
    large = max_exact + (np.log(np.maximum(d, 1) / max_exact) / math.log(REL_MAX_DIST / max_exact)
                         * (REL_BUCKETS - max_exact)).astype(np.int64)
    bucket = np.where(d < max_exact, d, np.minimum(large, REL_BUCKETS - 1))
    assert np.all(np.diff(bucket) >= 0)
    return [int(np.argmax(bucket >= j)) for j in range(REL_BUCKETS)]


_BUCKET_THR = _bucket_thresholds()


def _bias_from_dist(dist, tab_ref, head):
    val = jnp.full(dist.shape, tab_ref[0, head] * LOG2E, F32)
    for j in range(1, REL_BUCKETS):
        val = jnp.where(dist >= _BUCKET_THR[j], tab_ref[j, head] * LOG2E, val)
    return val


def _dot_nt(a, b):
    return lax.dot_general(a, b, (((1,), (1,)), ((), ())), preferred_element_type=F32)


def _dot_tn(a, b):
    return lax.dot_general(a, b, (((0,), (0,)), ((), ())), preferred_element_type=F32)


def _dot(a, b):
    return jnp.dot(a, b, preferred_element_type=F32)


def _lane_head(shape):
    return lax.shift_right_logical(lax.broadcasted_iota(jnp.int32, shape, 1), 6)


def _stack_heads(q):
    lh = _lane_head(q.shape)
    zero = jnp.zeros_like(q)
    return jnp.concatenate([jnp.where(lh == h, q, zero) for h in range(N_HEADS_BR)], axis=0)


def _merge_heads(o4, n):
    lh = _lane_head((n, BRANCH_W))
    out = jnp.zeros((n, BRANCH_W), F32)
    for h in range(N_HEADS_BR):
        out = jnp.where(lh == h, o4[h * n:(h + 1) * n, :], out)
    return out


def _same_head_ones():
    w = BRANCH_W
    same = (lax.shift_right_logical(lax.broadcasted_iota(jnp.int32, (w, w), 0), 6)
            == lax.shift_right_logical(lax.broadcasted_iota(jnp.int32, (w, w), 1), 6))
    return same, jnp.where(same, 1.0, 0.0).astype(BF16)


def _params(*sem):
    return pltpu.CompilerParams(dimension_semantics=sem, vmem_limit_bytes=V7X_VMEM_LIMIT)


_BF16_SLOTS = (0, 1, 2, 7, 8, 9)
_Q_SLOTS = (0, 7)
_MOBA_T_SLOTS = (0, 2)
_DIL_SLOTS = (7, 8, 9)
_RESIDUE_DILS = tuple(dil for _, dil in DIL_PAIRS if dil > 1)


def _inproj_kernel(x_ref, w_ref, wvt_ref, *refs):
    n_plain = N_IN_SLOTS + 1
    out_refs = refs[:n_plain]
    res_refs = refs[n_plain:n_plain + len(_DIL_SLOTS) * len(_RESIDUE_DILS)]
    slab_sc = refs[-1]
    xb = x_ref[...].astype(BF16)
    tm = xb.shape[0]
    blk = MOBA_BLOCK
    for s in range(N_IN_SLOTS):
        if s in _MOBA_T_SLOTS:
            ti = _MOBA_T_SLOTS.index(s)
            rt = _dot_nt(wvt_ref[ti * BRANCH_W:(ti + 1) * BRANCH_W, :], xb)
            if s in _Q_SLOTS:
                rt = rt * QK_SCALE
            rt = rt.astype(BF16)
            for j in range(tm // blk):
                out_refs[s][j] = rt[:, j * blk:(j + 1) * blk]
            continue
        r = _dot(xb, w_ref[:, s * BRANCH_W:(s + 1) * BRANCH_W])
        if s in _Q_SLOTS:
            r = r * QK_SCALE
        out_refs[s][...] = r.astype(out_refs[s].dtype)
        if s in _DIL_SLOTS:
            for half in range(BRANCH_W // LANES):
                slab_sc[half] = r[:, half * LANES:(half + 1) * LANES]
            for di, dil in enumerate(_RESIDUE_DILS):
                o_ref = res_refs[_DIL_SLOTS.index(s) * len(_RESIDUE_DILS) + di]
                for rr in range(dil):
                    for half in range(BRANCH_W // LANES):
                        c0 = rr * BRANCH_W + half * LANES
                        o_ref[:, c0:c0 + LANES] = slab_sc[half, pl.ds(rr, tm // dil, stride=dil), :].astype(BF16)
    g0 = N_IN_SLOTS * BRANCH_W
    out_refs[N_IN_SLOTS][...] = _dot(xb, w_ref[:, g0:])


def _inproj(xf, w, wvt):
    t, d = xf.shape
    tm = IN_TILE
    ncol = w.shape[1]
    blk = MOBA_BLOCK
    shapes = [jax.ShapeDtypeStruct((t, BRANCH_W), BF16 if s in _BF16_SLOTS else F32)
              for s in range(N_IN_SLOTS)]
    shapes.append(jax.ShapeDtypeStruct((t, ncol - N_IN_SLOTS * BRANCH_W), F32))
    out_specs = [pl.BlockSpec((tm, sh.shape[1]), lambda i: (i, 0)) for sh in shapes]
    for s in _MOBA_T_SLOTS:
        shapes[s] = jax.ShapeDtypeStruct((t // blk, BRANCH_W, blk), BF16)
        out_specs[s] = pl.BlockSpec((tm // blk, BRANCH_W, blk), lambda i: (i, 0, 0))
    for _ in _DIL_SLOTS:
        for dil in _RESIDUE_DILS:
            shapes.append(jax.ShapeDtypeStruct((t // dil, dil * BRANCH_W), BF16))
            out_specs.append(pl.BlockSpec((tm // dil, dil * BRANCH_W), lambda i: (i, 0)))
    outs = pl.pallas_call(
        _inproj_kernel,
        grid=(t // tm,),
        in_specs=[pl.BlockSpec((tm, d), lambda i: (i, 0)),
                  pl.BlockSpec((d, ncol), lambda i: (0, 0)),
                  pl.BlockSpec((len(_MOBA_T_SLOTS) * BRANCH_W, d), lambda i: (0, 0))],
        out_specs=out_specs,
        out_shape=shapes,
        scratch_shapes=[pltpu.VMEM((BRANCH_W // LANES, tm, LANES), F32)],
        compiler_params=_params("arbitrary"),
        name="inproj",
    )(xf, w, wvt)
    plain, res = outs[:N_IN_SLOTS + 1], outs[N_IN_SLOTS + 1:]
    nres = len(_RESIDUE_DILS)
    by_dil = {dil: [res[si * nres + di] for si in range(len(_DIL_SLOTS))]
              for di, dil in enumerate(_RESIDUE_DILS)}
    return plain, by_dil


_MOBA_BIAS_TILES = REL_MAX_DIST // MOBA_BLOCK + 2
_MASKED_OUT = 1e30


def _moba_kernel(tab_ref, q_ref, k_ref, vt_ref, o_ref,
                 bias_sc, kmean_sc, q4_sc, sel_sc, s0_sc, s1_sc, s2_sc, s3_sc,
                 mx0_sc, mx1_sc, mx2_sc, mx3_sc, acc_sc, m_sc, l_sc):
    b = pl.program_id(0)
    i = pl.program_id(1)
    blk = MOBA_BLOCK
    nkb = k_ref.shape[0] // blk
    nh = N_HEADS_BR
    hd = HEAD_DIM

    @pl.when((b == 0) & (i == 0))
    def _():
        key = lax.broadcasted_iota(jnp.int32, (blk, blk), 0)
        qry = lax.broadcasted_iota(jnp.int32, (blk, blk), 1)
        for m in range(_MOBA_BIAS_TILES):
            for h in range(nh):
                if m == _MOBA_BIAS_TILES - 1:
                    tile = jnp.full((blk, blk), tab_ref[REL_BUCKETS - 1, h] * LOG2E, F32)
                else:
                    tile = _bias_from_dist(jnp.maximum(m * blk + qry - key, 0), tab_ref, h)
                if m == 0:
                    tile = jnp.where(qry >= key, tile, NEG)
                bias_sc[m, :, h * blk:(h + 1) * blk] = tile.astype(BF16)

    @pl.when(i == 0)
    def _():
        for n in range(nkb):
            kb = k_ref[n * blk:(n + 1) * blk, :].astype(F32)
            kmean_sc[n:n + 1, :] = jnp.sum(kb, axis=0, keepdims=True) * (1.0 / blk)

    qt = q_ref[...]
    row_head = lax.shift_right_logical(lax.broadcasted_iota(jnp.int32, qt.shape, 0), 6)
    q4_sc[...] = jnp.concatenate(
        [jnp.where(row_head == h, qt, jnp.zeros_like(qt)) for h in range(nh)], axis=1)
    q4 = q4_sc[...]

    km = kmean_sc[...]
    km_hi = km.astype(BF16)
    km_lo = (km - km_hi.astype(F32)).astype(BF16)
    gate = _dot(km_hi, q4) + _dot(km_lo, q4)
    blk_id = lax.broadcasted_iota(jnp.int32, gate.shape, 0)
    gate = jnp.where(blk_id < i, gate, -jnp.inf)
    sel = jnp.where(blk_id == i, 1.0, 0.0)
    for _ in range(MOBA_TOPK):
        mx = jnp.max(gate, axis=0, keepdims=True)
        idx = jnp.min(jnp.where(gate == mx, blk_id, nkb), axis=0, keepdims=True)
        pick = (blk_id == idx) & (mx > -jnp.inf)
        sel = jnp.where(pick, 1.0, sel)
        gate = jnp.where(pick, -jnp.inf, gate)
    sel_sc[...] = sel

    m_sc[...] = jnp.full(m_sc.shape, NEG, F32)
    l_sc[...] = jnp.zeros(l_sc.shape, F32)
    acc_sc[...] = jnp.zeros(acc_sc.shape, F32)
    ones_rows = jnp.ones((BF16_ROWS, blk), BF16)

    def scores(n, dst_sc, dst_max_sc):
        k0 = pl.multiple_of(jnp.minimum(n, nkb - 1) * blk, blk)
        s = (_dot(k_ref[pl.ds(k0, blk), :], q4_sc[...]).astype(BF16)
             + bias_sc[jnp.clip(i - n, 0, _MOBA_BIAS_TILES - 1)])
        dst_sc[...] = s
        dst_max_sc[...] = jnp.max(s, axis=0, keepdims=True).astype(F32)

    def visit(n, src_sc, src_max_sc):
        nc = jnp.minimum(n, nkb - 1)
        chosen = sel_sc[pl.ds(nc, 1), :] > 0.0
        m_old = m_sc[...]
        m_new = jnp.maximum(m_old, jnp.where(chosen, src_max_sc[...], NEG))
        alpha = jnp.exp2(m_old - m_new)
        pb = jnp.exp2(src_sc[...] - jnp.where(chosen, m_new, _MASKED_OUT).astype(BF16))
        vt = vt_ref[nc]
        pv = jnp.concatenate(
            [_dot(jnp.concatenate([vt[h * hd:(h + 1) * hd, :], ones_rows], axis=0),
                  pb[:, h * blk:(h + 1) * blk]) for h in range(nh)], axis=1)
        acc_sc[...] = alpha * acc_sc[...] + pv[:hd]
        l_sc[...] = alpha * l_sc[...] + pv[hd:hd + 1]
        m_sc[...] = m_new

    bufs = ((s0_sc, mx0_sc), (s1_sc, mx1_sc), (s2_sc, mx2_sc), (s3_sc, mx3_sc))
    scores(0, *bufs[0])
    scores(1, *bufs[1])
    n_quads = (i + 1) // 4

    def quad_body(j, carry):
        n = 4 * j
        scores(n + 2, *bufs[2])
        visit(n, *bufs[0])
        scores(n + 3, *bufs[3])
        visit(n + 1, *bufs[1])
        scores(n + 4, *bufs[0])
        visit(n + 2, *bufs[2])
        scores(n + 5, *bufs[1])
        visit(n + 3, *bufs[3])
        return carry

    def pair_body(j, carry):
        n = 4 * n_quads + 2 * j
        visit(n, *bufs[0])
        scores(n + 2, *bufs[0])
        visit(n + 1, *bufs[1])
        scores(n + 3, *bufs[1])
        return carry

    lax.fori_loop(0, n_quads, quad_body, 0)
    lax.fori_loop(0, (i + 2 - 4 * n_quads) // 2, pair_body, 0)

    ot = acc_sc[...] / l_sc[...]
    drow = lax.broadcasted_iota(jnp.int32, (hd, BRANCH_W), 0)
    lane = lax.broadcasted_iota(jnp.int32, (hd, BRANCH_W), 1)
    out = jnp.zeros((blk, BRANCH_W), F32)
    for h in range(nh):
        place = jnp.where(lane == drow + h * hd, 1.0, 0.0).astype(BF16)
        piece = ot[:, h * blk:(h + 1) * blk]
        hi = piece.astype(BF16)
        lo = (piece - hi.astype(F32)).astype(BF16)
        out = out + _dot_tn(hi, place) + _dot_tn(lo, place)
    o_ref[...] = out


def _moba(tab, qt, k, vt):
    bsz, s, w = k.shape
    blk = MOBA_BLOCK
    nkb = s // blk
    nh = N_HEADS_BR
    return pl.pallas_call(
        _moba_kernel,
        grid=(bsz, nkb),
        in_specs=[pl.BlockSpec(memory_space=pltpu.SMEM),
                  pl.BlockSpec((None, None, w, blk), lambda b, i: (b, i, 0, 0)),
                  pl.BlockSpec((None, s, w), lambda b, i: (b, 0, 0)),
                  pl.BlockSpec((None, nkb, w, blk), lambda b, i: (b, 0, 0, 0))],
        out_specs=pl.BlockSpec((None, blk, w), lambda b, i: (b, i, 0)),
        out_shape=jax.ShapeDtypeStruct((bsz, s, w), F32),
        scratch_shapes=[pltpu.VMEM((_MOBA_BIAS_TILES, blk, nh * blk), BF16),
                        pltpu.VMEM((nkb, w), F32),
                        pltpu.VMEM((w, nh * blk), BF16),
                        pltpu.VMEM((nkb, nh * blk), F32),
                        *[pltpu.VMEM((blk, nh * blk), BF16) for _ in range(4)],
                        *[pltpu.VMEM((1, nh * blk), F32) for _ in range(4)],
                        pltpu.VMEM((HEAD_DIM, nh * blk), F32),
                        pltpu.VMEM((1, nh * blk), F32),
                        pltpu.VMEM((1, nh * blk), F32)],
        compiler_params=_params("arbitrary", "arbitrary"),
        name="moba",
    )(tab, qt, k, vt)


def _dil_kernel(tab_ref, q_ref, kc_ref, kp_ref, vc_ref, vp_ref, o_ref, lse_ref, bias_sc, *, dil, tq):
    nb = DIL_BLOCK
    nh = N_HEADS_BR
    t = pl.program_id(2)
    first = (pl.program_id(0) == 0) & (pl.program_id(1) == 0) & (t == 0)

    @pl.when(first)
    def _():
        row = lax.broadcasted_iota(jnp.int32, (nb, 2 * nb), 0)
        col = lax.broadcasted_iota(jnp.int32, (nb, 2 * nb), 1)
        dist_sub = nb + row - col
        band = (dist_sub >= 0) & (dist_sub <= nb)
        for h in range(nh):
            tile = _bias_from_dist(jnp.maximum(dist_sub, 0) * dil, tab_ref, nh + h)
            bias_sc[h * nb:(h + 1) * nb, :] = jnp.where(band, tile, NEG)

    for jb in range(tq // nb):
        lo, hi = jb * nb, (jb + 1) * nb
        q4 = _stack_heads(q_ref[lo:hi, :])
        if jb == 0:
            kprev, vprev = kp_ref[tq - nb:tq, :], vp_ref[tq - nb:tq, :]
        else:
            kprev, vprev = kc_ref[lo - nb:lo, :], vc_ref[lo - nb:lo, :]
        keys = jnp.concatenate([kprev, kc_ref[lo:hi, :]], axis=0)
        vals = jnp.concatenate([vprev, vc_ref[lo:hi, :]], axis=0)
        s = _dot_nt(q4, keys) + bias_sc[...]
        if jb == 0:
            col = lax.broadcasted_iota(jnp.int32, s.shape, 1)
            s = jnp.where((col >= nb) | (t > 0), s, NEG)
        m = jnp.max(s, axis=-1, keepdims=True)
        p = jnp.exp2(s - m)
        l = jnp.sum(p, axis=-1, keepdims=True)
        o4 = _dot(p.astype(BF16), vals) / l
        lse4 = jnp.broadcast_to((m + jnp.log2(l)) * LN2, o4.shape)
        o_ref[lo:hi, :] = _merge_heads(o4, nb)
        lse_ref[lo:hi, :] = _merge_heads(lse4, nb)


def _dilated(tab, q, k, v, dil, bsz):
    rows, width = q.shape
    w = BRANCH_W
    sub_len = rows // bsz
    tq = min(512, sub_len)
    nt = sub_len // tq
    view = lambda a: a.reshape(bsz, sub_len, width)
    cur = pl.BlockSpec((None, tq, w), lambda b, r, t: (b, t, r))
    prev = pl.BlockSpec((None, tq, w), lambda b, r, t: (b, jnp.maximum(t - 1, 0), r))
    out_shape = jax.ShapeDtypeStruct((bsz, sub_len, width), F32)
    o, lse = pl.pallas_call(
        functools.partial(_dil_kernel, dil=dil, tq=tq),
        grid=(bsz, dil, nt),
        in_specs=[pl.BlockSpec(memory_space=pltpu.SMEM), cur, cur, prev, cur, prev],
        out_specs=[cur, cur],
        out_shape=[out_shape, out_shape],
        scratch_shapes=[pltpu.VMEM((N_HEADS_BR * DIL_BLOCK, 2 * DIL_BLOCK), F32)],
        compiler_params=_params("arbitrary", "arbitrary", "arbitrary"),
        name=f"dilated{dil}",
    )(tab, view(q), view(k), view(k), view(v), view(v))
    return o.reshape(rows, width), lse.reshape(rows, width)


_S5_STEPS = (1, 2, 4)


def _s5_kernel(u_ref, wb_ref, coef_ref, wc_ref, d_ref, gw_ref, gb_ref, o_ref, h_sc, carry_sc):
    ts = u_ref.shape[0]
    nl = S5_LANES

    @pl.when(pl.program_id(1) == 0)
    def _():
        carry_sc[...] = jnp.zeros_like(carry_sc)

    u = u_ref[...]
    h_sc[...] = _dot(u.astype(BF16), wb_ref[...])

    def cfma(re, im, cr, ci, sre, sim):
        return re + cr * sre - ci * sim, im + cr * sim + ci * sre

    def body(g, carry):
        cre, cim = carry
        r0 = pl.multiple_of(g * SUBLANES, SUBLANES)
        re = h_sc[pl.ds(r0, SUBLANES), :nl]
        im = h_sc[pl.ds(r0, SUBLANES), nl:]
        for si, k in enumerate(_S5_STEPS):
            re, im = cfma(re, im, coef_ref[si, 0], coef_ref[si, 1],
                          pltpu.roll(re, k, axis=0), pltpu.roll(im, k, axis=0))
        last = len(_S5_STEPS)
        re, im = cfma(re, im, coef_ref[last, 0], coef_ref[last, 1],
                      jnp.broadcast_to(cre, re.shape), jnp.broadcast_to(cim, im.shape))
        h_sc[pl.ds(r0, SUBLANES), :nl] = re
        h_sc[pl.ds(r0, SUBLANES), nl:] = im
        return re[SUBLANES - 1:, :], im[SUBLANES - 1:, :]

    cre, cim = lax.fori_loop(0, ts // SUBLANES, body, (carry_sc[0:1, :nl], carry_sc[0:1, nl:]))
    carry_sc[0:1, :nl] = cre
    carry_sc[0:1, nl:] = cim

    y = _dot(h_sc[...].astype(BF16), wc_ref[...]) + d_ref[...] * u
    y = 0.5 * y * (1.0 + jnp.tanh(math.sqrt(2.0 / math.pi) * (y + 0.044715 * (y * y * y))))
    z = _dot(y.astype(BF16), gw_ref[...]) + gb_ref[...]
    o_ref[...] = y * jax.nn.sigmoid(z)


def _s5_weights(a_re, a_im, log_dt, b_re, b_im, c_re, c_im):
    g, p, c = b_re.shape
    ar, ai = a_re.astype(F32), a_im.astype(F32)
    dt = jnp.exp(log_dt.astype(F32))[:, None]

    def abar_pow(k):
        mag = jnp.exp(k * dt * ar)
        return (mag * jnp.cos(k * dt * ai)).reshape(-1), (mag * jnp.sin(k * dt * ai)).reshape(-1)

    abar_re, abar_im = jnp.exp(dt * ar) * jnp.cos(dt * ai), jnp.exp(dt * ar) * jnp.sin(dt * ai)
    nr, ni = abar_re - 1.0, abar_im
    den = ar * ar + ai * ai
    zr = (nr * ar + ni * ai) / den
    zi = (ni * ar - nr * ai) / den
    br, bi = b_re.astype(F32), b_im.astype(F32)
    bbar_re = zr[..., None] * br - zi[..., None] * bi
    bbar_im = zr[..., None] * bi + zi[..., None] * br
    eye = jnp.eye(g, dtype=F32)
    blockdiag_in = lambda m: jnp.einsum('gpc,gh->gchp', m, eye).reshape(g * c, g * p)
    wb = jnp.concatenate([blockdiag_in(bbar_re), blockdiag_in(bbar_im)], axis=1)
    blockdiag_out = lambda m: jnp.einsum('gcp,gh->gphc', m, eye).reshape(g * p, g * c)
    wc = jnp.concatenate([blockdiag_out(c_re.astype(F32)), -blockdiag_out(c_im.astype(F32))], axis=0)
    rows = jnp.arange(SUBLANES)[:, None]
    coefs = []
    for k in _S5_STEPS:
        pr, pi = abar_pow(k)
        coefs.append(jnp.stack([jnp.where(rows >= k, pr[None, :], 0.0),
                                jnp.where(rows >= k, pi[None, :], 0.0)]))
    per_row = [abar_pow(k + 1) for k in range(SUBLANES)]
    coefs.append(jnp.stack([jnp.stack([r for r, _ in per_row]), jnp.stack([i for _, i in per_row])]))
    return wb.astype(BF16), jnp.stack(coefs), wc.astype(BF16)


def _s5(u, wb, coef, wc, d_skip, glu_w, glu_b):
    bsz, s, w = u.shape
    ts = min(512, s)
    const = lambda shape: pl.BlockSpec(shape, lambda b, t: (0,) * len(shape))
    return pl.pallas_call(
        _s5_kernel,
        grid=(bsz, s // ts),
        in_specs=[pl.BlockSpec((None, ts, w), lambda b, t: (b, t, 0)),
                  const(wb.shape), const(coef.shape), const(wc.shape),
                  const((1, w)), const(glu_w.shape), const((1, w))],
        out_specs=pl.BlockSpec((None, ts, w), lambda b, t: (b, t, 0)),
        out_shape=jax.ShapeDtypeStruct((bsz, s, w), F32),
        scratch_shapes=[pltpu.VMEM((ts, 2 * S5_LANES), F32),
                        pltpu.VMEM((SUBLANES, 2 * S5_LANES), F32)],
        compiler_params=_params("arbitrary", "arbitrary"),
        name="s5",
    )(u, wb, coef, wc, d_skip.reshape(1, w), glu_w, glu_b.reshape(1, w))


HGRN_SEQS = 4


def _hgrn_kernel(lb_ref, q_ref, f_ref, i_ref, o_ref, st_sc):
    nseq, th, w = q_ref.shape
    c = HGRN_SUB

    @pl.when(pl.program_id(1) == 0)
    def _():
        st_sc[...] = jnp.zeros_like(st_sc)

    lb = lb_ref[...]
    rowi = lax.broadcasted_iota(jnp.int32, (c, w), 0)
    same_head, head_ones = _same_head_ones()

    def step(bi, r0):
        q = q_ref[bi, pl.ds(r0, c), :]
        qf = q * jax.nn.sigmoid(q) * (HEAD_DIM ** -0.5)
        f = lb + (1.0 - lb) * jax.nn.sigmoid(f_ref[bi, pl.ds(r0, c), :])
        kk = 1.0 - f
        v = i_ref[bi, pl.ds(r0, c), :]
        b = jnp.log(f)
        k = 1
        while k < c:
            b = b + jnp.where(rowi >= k, pltpu.roll(b, k, axis=0), 0.0)
            k *= 2
        e = jnp.concatenate(
            [jnp.exp(jnp.where(rowi >= s, b - b[s:s + 1, :], NEG)) * (qf * kk[s:s + 1, :])
             for s in range(c)], axis=0)
        wgt = _dot(e.astype(BF16), head_ones)
        o = jnp.zeros((c, w), F32)
        for s in range(c):
            o = o + wgt[s * c:(s + 1) * c, :] * v[s:s + 1, :]
        st = st_sc[bi]
        o = o + _dot_nt((qf * jnp.exp(b)).astype(BF16), st.astype(BF16))
        b_last = b[c - 1:c, :]
        upd = _dot_tn(v.astype(BF16), (kk * jnp.exp(b_last - b)).astype(BF16))
        st_sc[bi] = jnp.exp(b_last) * st + jnp.where(same_head, upd, 0.0)
        o_ref[bi, pl.ds(r0, c), :] = o

    def body(ci, carry):
        r0 = pl.multiple_of(ci * c, c)
        for bi in range(nseq):
            step(bi, r0)
        return carry

    lax.fori_loop(0, th // c, body, 0)


def _hgrn(lb, q, f, i):
    bsz, s, w = q.shape
    th = min(512, s)
    nseq = HGRN_SEQS if bsz % HGRN_SEQS == 0 else 1
    tile = pl.BlockSpec((nseq, th, w), lambda b, t: (b, t, 0))
    return pl.pallas_call(
        _hgrn_kernel,
        grid=(bsz // nseq, s // th),
        in_specs=[pl.BlockSpec((1, w), lambda b, t: (0, 0)), tile, tile, tile],
        out_specs=tile,
        out_shape=jax.ShapeDtypeStruct((bsz, s, w), F32),
        scratch_shapes=[pltpu.VMEM((nseq, w, w), F32)],
        compiler_params=_params("arbitrary", "arbitrary"),
        name="hgrn2",
    )(lb.reshape(1, w), q, f, i)


def _outproj_kernel(ya_ref, yb_ref, yc_ref, o1_ref, l1_ref, *rest, alpha):
    nres = len(_RESIDUE_DILS)
    res_refs = rest[:2 * nres]
    gates_ref, x_ref, w_ref, gain_ref, lng_ref, lnb_ref, out_ref, slab_sc = rest[2 * nres:]
    tm = x_ref.shape[0]
    halves = BRANCH_W // LANES

    def token_order(ref, dil):
        for rr in range(dil):
            for half in range(halves):
                c0 = rr * BRANCH_W + half * LANES
                slab_sc[half, pl.ds(rr, tm // dil, stride=dil), :] = ref[:, c0:c0 + LANES]
        return jnp.concatenate([slab_sc[half] for half in range(halves)], axis=1)

    outs, lses = [o1_ref[...]], [l1_ref[...]]
    for di, dil in enumerate(_RESIDUE_DILS):
        outs.append(token_order(res_refs[2 * di], dil))
        lses.append(token_order(res_refs[2 * di + 1], dil))
    lm = functools.reduce(jnp.maximum, lses)
    es = [jnp.exp(l - lm) for l in lses]
    yd = sum(e * o for e, o in zip(es, outs)) / sum(es)

    _, head_ones = _same_head_ones()
    normed = []
    for y in (ya_ref[...], yb_ref[...], yc_ref[...], yd):
        y2 = y * y
        hi = y2.astype(BF16)
        lo = (y2 - hi.astype(F32)).astype(BF16)
        ss = _dot(hi, head_ones) + _dot(lo, head_ones)
        normed.append(y * lax.rsqrt(ss * (1.0 / HEAD_DIM) + RMS_EPS))
    g = gates_ref[...]
    y = jnp.concatenate(normed, axis=-1) * gain_ref[...] * (g * jax.nn.sigmoid(g))
    r = alpha * x_ref[...] + _dot(y.astype(BF16), w_ref[...])
    mu = jnp.mean(r, axis=-1, keepdims=True)
    rc = r - mu
    var = jnp.mean(rc * rc, axis=-1, keepdims=True)
    out_ref[...] = rc * lax.rsqrt(var + LN_EPS) * lng_ref[...] + lnb_ref[...]


def _outproj(branches, residue_major, gates, xf, w, gain, ln_g, ln_b, alpha):
    t, d = xf.shape
    tm = OUT_TILE
    wide = pl.BlockSpec((tm, d), lambda i: (i, 0))
    narrow = pl.BlockSpec((tm, BRANCH_W), lambda i: (i, 0))
    vec = pl.BlockSpec((1, d), lambda i: (0, 0))
    res_specs, res_args = [], []
    for dil, (o, lse) in zip(_RESIDUE_DILS, residue_major):
        spec = pl.BlockSpec((tm // dil, dil * BRANCH_W), lambda i: (i, 0))
        res_specs += [spec, spec]
        res_args += [o, lse]
    return pl.pallas_call(
        functools.partial(_outproj_kernel, alpha=alpha),
        grid=(t // tm,),
        in_specs=[narrow] * 5 + res_specs + [wide, wide, pl.BlockSpec((d, d), lambda i: (0, 0)), vec, vec, vec],
        out_specs=wide,
        out_shape=jax.ShapeDtypeStruct((t, d), F32),
        scratch_shapes=[pltpu.VMEM((BRANCH_W // LANES, tm, LANES), F32)],
        compiler_params=_params("arbitrary"),
        name="outproj",
    )(*branches, *res_args, gates, xf, w, gain.reshape(1, d), ln_g.reshape(1, d), ln_b.reshape(1, d))


def kernel(x, w_in, rel_bias, s5_a_re, s5_a_im, s5_log_dt, s5_b_re, s5_b_im, s5_c_re, s5_c_im,
           s5_d, s5_glu_w, s5_glu_b, hgrn_lower, branch_gain, w_out, ln_g, ln_b):
    bsz, s, d = x.shape
    depth = w_in.shape[0]
    t = bsz * s
    w = BRANCH_W
    alpha = (2 * depth) ** 0.25
    p_lb = jax.nn.softmax(hgrn_lower.astype(F32), axis=0)
    lb_all = jnp.cumsum(p_lb, axis=0) - p_lb[0]
    tab = rel_bias.astype(F32)
    w_in_b = w_in.astype(BF16)
    w_out_b = w_out.astype(BF16)
    glu_w_b = s5_glu_w.astype(BF16)

    xf = x.reshape(t, d).astype(F32)
    for l in range(depth):
        wvt = jnp.concatenate([w_in_b[l][:, ts * w:(ts + 1) * w].T for ts in _MOBA_T_SLOTS], axis=0)
        plain, by_dil = _inproj(xf, w_in_b[l], wvt)
        qat, ka, vat, us, qc, fc, ic, qd, kd, vd, gates = plain
        seq = lambda a: a.reshape(bsz, s, w)
        per_block = lambda a: a.reshape(bsz, s // MOBA_BLOCK, w, MOBA_BLOCK)
        ya = _moba(tab, per_block(qat), seq(ka), per_block(vat))
        wb, coef, wc = _s5_weights(s5_a_re[l], s5_a_im[l], s5_log_dt[l], s5_b_re[l], s5_b_im[l],
                                   s5_c_re[l], s5_c_im[l])
        yb = _s5(seq(us), wb, coef, wc, s5_d[l].astype(F32), glu_w_b[l], s5_glu_b[l].astype(F32))
        yc = _hgrn(lb_all[l], seq(qc), seq(fc), seq(ic))
        o1, lse1 = _dilated(tab, qd, kd, vd, 1, bsz)
        residue_major = [_dilated(tab, *by_dil[dil], dil, bsz) for dil in _RESIDUE_DILS]
        flat = lambda a: a.reshape(t, w)
        xf = _outproj([flat(ya), flat(yb), flat(yc), o1, lse1], residue_major, gates, xf, w_out_b[l],
                      branch_gain[l].astype(F32), ln_g[l].astype(F32), ln_b[l].astype(F32), alpha)
    return xf.reshape(bsz, s, d).astype(x.dtype)
```
